```python
import jax, jax.numpy as jnp
from jax import lax
import numpy as np

D_MODEL = 1024
BATCH = 4
SEQ = 4096
DEPTH = 4
DEC_BATCH = 128
DEC_SEQ = 1
PAST_LEN = 2048
PAGE_SIZE = 128

N_HEADS = 8
HEAD_DIM = 64
D_ATTN = N_HEADS * HEAD_DIM
D_POOL = D_MODEL // 2
POOL_WINDOWS = (2, 4, 8, 16)
N_POOL_GROUPS = len(POOL_WINDOWS)
POOL_GROUP = D_POOL // N_POOL_GROUPS
POOL_STATE = max(POOL_WINDOWS) - 1
D_FF = 2816
CONV_WIDTH = 3
CONV_STATE = CONV_WIDTH - 1
Q_BLOCK = 128
D_IN = 3 * D_ATTN + N_HEADS + D_POOL + 2 * D_MODEL
DEEPNORM_ALPHA = (2 * DEPTH) ** 0.25
DEEPNORM_BETA = (8 * DEPTH) ** -0.25
LN_EPS = 1e-5
NEG_INF = -1e30
FORGET_BIAS = 3.0

kernel_name = "fox_pool_convffn_deepnorm_step"


def layer_norm(x, g, b):
    xf = x.astype(jnp.float32)
    mu = jnp.mean(xf, axis=-1, keepdims=True)
    var = jnp.mean(jnp.square(xf - mu), axis=-1, keepdims=True)
    return ((xf - mu) * lax.rsqrt(var + LN_EPS) * g.astype(jnp.float32) + b.astype(jnp.float32)).astype(x.dtype)


def attend_block(q, k, v, cum_q, cum_k, q_pos, k_pos):
    s = jnp.einsum('bqhd,bkhd->bhqk', q, k).astype(jnp.float32) * (HEAD_DIM ** -0.5)
    s = s + cum_q[..., :, None] - cum_k[..., None, :]
    s = jnp.where(k_pos[None, :] <= q_pos[:, None], s, NEG_INF)
    p = jax.nn.softmax(s, axis=-1).astype(v.dtype)
    return jnp.einsum('bhqk,bkhd->bqhd', p, v)


def prompt_attention(q, k, v, logf):
    b, s, h, d = q.shape
    nb = s // Q_BLOCK
    cum_k = jnp.cumsum(logf, axis=1).transpose(0, 2, 1)
    k_pos = jnp.arange(s)
    qb = q.reshape(b, nb, Q_BLOCK, h, d).transpose(1, 0, 2, 3, 4)
    cq = cum_k.reshape(b, h, nb, Q_BLOCK).transpose(2, 0, 1, 3)
    starts = jnp.arange(nb) * Q_BLOCK

    def block(args):
        qi, ci, st = args
        return attend_block(qi, k, v, ci, cum_k, st + jnp.arange(Q_BLOCK), k_pos)

    o = lax.map(block, (qb, cq, starts))
    return o.transpose(1, 0, 2, 3, 4).reshape(b, s, h * d)


def sample_attention(q, k, v, logf, k_past, v_past, logf_past):
    b, t, h, d = q.shape
    p = k_past.shape[1]
    k_all = jnp.concatenate([k_past.astype(k.dtype), k], axis=1)
    v_all = jnp.concatenate([v_past.astype(v.dtype), v], axis=1)
    f_all = jnp.concatenate([logf_past.astype(jnp.float32), logf], axis=1)
    cum = jnp.cumsum(f_all, axis=1).transpose(0, 2, 1)
    o = attend_block(q, k_all, v_all, cum[..., p:], cum, p + jnp.arange(t), jnp.arange(p + t))
    return o.reshape(b, t, h * d)


def pool_branch(u_ext, pos, w_pool, pool_scale):
    t = pos.shape[0]
    uf = u_ext.astype(jnp.float32)
    c = jnp.cumsum(uf, axis=1)
    c = jnp.concatenate([jnp.zeros_like(c[:, :1]), c], axis=1)
    lo = POOL_STATE + 1
    outs = []
    for g, w in enumerate(POOL_WINDOWS):
        sl = slice(g * POOL_GROUP, (g + 1) * POOL_GROUP)
        cg = c[..., sl]
        win = cg[:, lo:lo + t] - cg[:, lo - w:lo - w + t]
        cnt = jnp.minimum(pos + 1, w).astype(jnp.float32)[None, :, None]
        pooled = win / cnt - uf[:, POOL_STATE:, sl]
        outs.append(jnp.einsum('btc,cd->btd', pooled.astype(u_ext.dtype), w_pool[g]))
    return jnp.concatenate(outs, axis=-1) * pool_scale


def conv_ffn(x, prefix, w_up, conv_w, conv_b, w_down):
    t = x.shape[1]
    up = x @ w_up
    ext = jnp.concatenate([prefix.astype(up.dtype), up], axis=1)
    h = conv_b + ext[:, CONV_STATE:CONV_STATE + t] * conv_w[CONV_STATE]
    for i in range(CONV_STATE):
        h = h + ext[:, i:i + t] * conv_w[i]
    a, gt = jnp.split(h, 2, axis=-1)
    y = (jax.nn.gelu(a) * gt) @ w_down
    return y, ext[:, -CONV_STATE:]


def trunk_layer(x, pos, attend, pool_prefix, conv_prefix, w_in, b_forget, w_pool, pool_scale,
                w_proj_a, w_proj_b, w_out, ln1_g, ln1_b, w_up, conv_w, conv_b, w_down, ln2_g, ln2_b):
    b, t, _ = x.shape
    z = x @ w_in
    o0 = 0
    q = z[..., o0:o0 + D_ATTN].reshape(b, t, N_HEADS, HEAD_DIM); o0 += D_ATTN
    k = z[..., o0:o0 + D_ATTN].reshape(b, t, N_HEADS, HEAD_DIM); o0 += D_ATTN
    v = z[..., o0:o0 + D_ATTN].reshape(b, t, N_HEADS, HEAD_DIM); o0 += D_ATTN
    f = z[..., o0:o0 + N_HEADS]; o0 += N_HEADS
    u = z[..., o0:o0 + D_POOL]; o0 += D_POOL
    ga = z[..., o0:o0 + D_MODEL]; o0 += D_MODEL
    gb = z[..., o0:o0 + D_MODEL]
    logf = jax.nn.log_sigmoid((f + b_forget).astype(jnp.float32))

    o_a = attend(q, k, v, logf)
    u_ext = jnp.concatenate([pool_prefix.astype(u.dtype), u], axis=1)
    o_b = pool_branch(u_ext, pos, w_pool, pool_scale)
    merged = jax.nn.sigmoid(ga) * (o_a @ w_proj_a) + jax.nn.sigmoid(gb) * (o_b @ w_proj_b)
    x = layer_norm(DEEPNORM_ALPHA * x + merged @ w_out, ln1_g, ln1_b)

    y, conv_state = conv_ffn(x, conv_prefix, w_up, conv_w, conv_b, w_down)
    x = layer_norm(DEEPNORM_ALPHA * x + y, ln2_g, ln2_b)
    return x, k, v, logf, u_ext[:, -POOL_STATE:], conv_state


def setup_inputs(seed: int = 0) -> dict:
    key = jax.random.key(seed)
    ks = jax.random.split(key, 24)
    n_pages = PAST_LEN // PAGE_SIZE
    n_used = DEC_BATCH * n_pages
    n_pool = n_used + n_used // 4

    def nrm(k, shape, s):
        return s * jax.random.normal(k, shape, jnp.float32)

    x_prompt = nrm(ks[0], (BATCH, SEQ, D_MODEL), 1.0)
    x_sample = nrm(ks[1], (DEC_BATCH, DEC_SEQ, D_MODEL), 1.0)
    cache_k = nrm(ks[2], (DEPTH, n_pool, PAGE_SIZE, N_HEADS, HEAD_DIM), 1.0)
    cache_v = nrm(ks[3], (DEPTH, n_pool, PAGE_SIZE, N_HEADS, HEAD_DIM), 1.0)
    cache_logf = jax.nn.log_sigmoid(FORGET_BIAS + nrm(ks[4], (DEPTH, n_pool, PAGE_SIZE, N_HEADS), 0.5))
    state_pool = nrm(ks[5], (DEPTH, DEC_BATCH, POOL_STATE, D_POOL), 1.0)
    state_conv = nrm(ks[6], (DEPTH, DEC_BATCH, CONV_STATE, 2 * D_FF), 1.0)
    perm = jax.random.permutation(ks[7], n_pool)
    page_table = perm[:n_used].reshape(DEC_BATCH, n_pages).astype(jnp.int32)

    w_in = nrm(ks[8], (DEPTH, D_MODEL, D_IN), D_MODEL ** -0.5)
    b_forget = FORGET_BIAS + nrm(ks[9], (DEPTH, N_HEADS), 0.1)
    w_pool = nrm(ks[10], (DEPTH, N_POOL_GROUPS, POOL_GROUP, POOL_GROUP), POOL_GROUP ** -0.5)
    pool_scale = 1.0 + nrm(ks[11], (DEPTH, D_POOL), 0.1)
    w_proj_a = nrm(ks[12], (DEPTH, D_ATTN, D_MODEL), D_ATTN ** -0.5)
    w_proj_b = nrm(ks[13], (DEPTH, D_POOL, D_MODEL), D_POOL ** -0.5)
    w_out = nrm(ks[14], (DEPTH, D_MODEL, D_MODEL), DEEPNORM_BETA * D_MODEL ** -0.5)
    ln1_g = 1.0 + nrm(ks[15], (DEPTH, D_MODEL), 0.05)
    ln1_b = nrm(ks[16], (DEPTH, D_MODEL), 0.02)
    w_up = nrm(ks[17], (DEPTH, D_MODEL, 2 * D_FF), D_MODEL ** -0.5)
    conv_w = nrm(ks[18], (DEPTH, CONV_WIDTH, 2 * D_FF), CONV_WIDTH ** -0.5)
    conv_b = nrm(ks[19], (DEPTH, 2 * D_FF), 0.02)
    w_down = nrm(ks[20], (DEPTH, D_FF, D_MODEL), DEEPNORM_BETA * D_FF ** -0.5)
    ln2_g = 1.0 + nrm(ks[21], (DEPTH, D_MODEL), 0.05)
    ln2_b = nrm(ks[22], (DEPTH, D_MODEL), 0.02)
    return {"x_prompt": x_prompt, "x_sample": x_sample, "cache_k": cache_k, "cache_v": cache_v,
            "cache_logf": cache_logf, "state_pool": state_pool, "state_conv": state_conv,
            "page_table": page_table, "w_in": w_in, "b_forget": b_forget, "w_pool": w_pool,
            "pool_scale": pool_scale, "w_proj_a": w_proj_a, "w_proj_b": w_proj_b, "w_out": w_out,
            "ln1_g": ln1_g, "ln1_b": ln1_b, "w_up": w_up, "conv_w": conv_w, "conv_b": conv_b,
            "w_down": w_down, "ln2_g": ln2_g, "ln2_b": ln2_b}


def reference(x_prompt, x_sample, cache_k, cache_v, cache_logf, state_pool, state_conv, page_table,
              w_in, b_forget, w_pool, pool_scale, w_proj_a, w_proj_b, w_out, ln1_g, ln1_b,
              w_up, conv_w, conv_b, w_down, ln2_g, ln2_b):
    n_pages = page_table.shape[1]
    past_len = n_pages * PAGE_SIZE
    bp, sp, _ = x_prompt.shape
    bs, ss, _ = x_sample.shape
    pos_p = jnp.arange(sp)
    pos_s = past_len + jnp.arange(ss)
    pool_zero = jnp.zeros((bp, POOL_STATE, D_POOL), x_prompt.dtype)
    conv_zero = jnp.zeros((bp, CONV_STATE, 2 * D_FF), x_prompt.dtype)

    xp, xs = x_prompt, x_sample
    kp, vp, fp, pp, cp = [], [], [], [], []
    ksm, vsm, fsm, psm, csm = [], [], [], [], []
    for l in range(DEPTH):
        wl = (w_in[l], b_forget[l], w_pool[l], pool_scale[l], w_proj_a[l], w_proj_b[l], w_out[l],
              ln1_g[l], ln1_b[l], w_up[l], conv_w[l], conv_b[l], w_down[l], ln2_g[l], ln2_b[l])
        xp, k_, v_, f_, pst, cst = trunk_layer(xp, pos_p, prompt_attention, pool_zero, conv_zero, *wl)
        kp.append(k_); vp.append(v_); fp.append(f_); pp.append(pst); cp.append(cst)

        k_past = cache_k[l][page_table].reshape(bs, past_len, N_HEADS, HEAD_DIM)
        v_past = cache_v[l][page_table].reshape(bs, past_len, N_HEADS, HEAD_DIM)
        f_past = cache_logf[l][page_table].reshape(bs, past_len, N_HEADS)

        def attend_s(q, k, v, logf, k_past=k_past, v_past=v_past, f_past=f_past):
            return sample_attention(q, k, v, logf, k_past, v_past, f_past)

        xs, k_, v_, f_, pst, cst = trunk_layer(xs, pos_s, attend_s, state_pool[l], state_conv[l], *wl)
        ksm.append(k_); vsm.append(v_); fsm.append(f_); psm.append(pst); csm.append(cst)

    return (xp, xs,
            jnp.stack(kp), jnp.stack(vp), jnp.stack(fp), jnp.stack(pp), jnp.stack(cp),
            jnp.stack(ksm), jnp.stack(vsm), jnp.stack(fsm), jnp.stack(psm), jnp.stack(csm))
```

```python
import functools

import jax
import jax.numpy as jnp
from jax import lax
from jax.experimental import pallas as pl
from jax.experimental.pallas import tpu as pltpu

D_MODEL = 1024
N_HEADS = 8
HEAD_DIM = 64
D_ATTN = N_HEADS * HEAD_DIM
D_POOL = D_MODEL // 2
POOL_WINDOWS = (2, 4, 8, 16)
POOL_GROUP = D_POOL // len(POOL_WINDOWS)
POOL_STATE = max(POOL_WINDOWS) - 1
POOL_HALO = 16
D_FF = 2816
CONV_STATE = 2
PAGE_SIZE = 128
LN_EPS = 1e-5
NEG_INF = -1e30
LANE = 128
F_PAD = 16

C_Q, C_K, C_V = 0, D_ATTN, 2 * D_ATTN
C_F = 3 * D_ATTN
C_U = C_F + LANE
C_G = C_U + D_POOL
C_END = C_G + 2 * D_MODEL

TM = 512
TQ = 512
FF_CHUNK = 256
VMEM_LIMIT = 56 * 1024 * 1024

f32 = jnp.float32
bf16 = jnp.bfloat16
NT_DIMS = (((1,), (1,)), ((), ()))
TN_DIMS = (((0,), (0,)), ((), ()))


def _dot(a, b):
    return jnp.dot(a, b, preferred_element_type=f32)


def _log_sigmoid(x):
    return jnp.minimum(x, 0.0) - jnp.log1p(jnp.exp(-jnp.abs(x)))


def _sigmoid(x):
    return 1.0 / (1.0 + jnp.exp(-x))


def _gelu_tanh(x):
    return 0.5 * x * (1.0 + jnp.tanh(0.7978845608028654 * (x + 0.044715 * (x * x * x))))


def _layer_norm(y, g, b):
    mu = jnp.mean(y, axis=-1, keepdims=True)
    d = y - mu
    var = jnp.mean(d * d, axis=-1, keepdims=True)
    return d * lax.rsqrt(var + LN_EPS) * g + b


def _split3(x):
    hi = x.astype(bf16)
    r1 = x - hi.astype(f32)
    mid = r1.astype(bf16)
    lo = (r1 - mid.astype(f32)).astype(bf16)
    return hi, mid, lo


def _const_spec(shape):
    nd = len(shape)
    return pl.BlockSpec(shape, lambda *_: (0,) * nd, pipeline_mode=pl.Buffered(1))


def _full_spec(shape):
    nd = len(shape)
    return pl.BlockSpec(shape, lambda *_: (0,) * nd)


def _params(semantics):
    return pltpu.CompilerParams(dimension_semantics=semantics, vmem_limit_bytes=VMEM_LIMIT)


def _proj_prompt_kernel(x_ref, w_ref, wt_ref, brow_ref, bcol_ref,
                        qT_ref, k_ref, kb_ref, v_ref, vT_ref, logf_ref, cumc_ref, cumr_ref, u_ref, sg_ref,
                        carry_c, carry_r, *, tm):
    i = pl.program_id(1)

    @pl.when(i == 0)
    def _():
        carry_c[...] = jnp.zeros_like(carry_c)
        carry_r[...] = jnp.zeros_like(carry_r)

    xb = x_ref[...].astype(bf16)
    zt = lax.dot_general(wt_ref[...], xb, NT_DIMS, preferred_element_type=f32)
    qT_ref[...] = zt[0:D_ATTN].astype(bf16)
    vT_ref[...] = zt[D_ATTN:2 * D_ATTN].astype(bf16)
    lf_row = _log_sigmoid(zt[2 * D_ATTN:2 * D_ATTN + F_PAD] + bcol_ref[...])

    k = _dot(xb, w_ref[:, C_K:C_V])
    k_ref[...] = k
    kb_ref[...] = k.astype(bf16)
    v_ref[...] = _dot(xb, w_ref[:, C_V:C_F])
    lf_col = _log_sigmoid(_dot(xb, w_ref[:, C_F:C_U]) + brow_ref[...])
    logf_ref[...] = lf_col[:, 0:N_HEADS]

    r = lax.broadcasted_iota(jnp.int32, (tm, tm), 0)
    c = lax.broadcasted_iota(jnp.int32, (tm, tm), 1)
    lower = jnp.where(c <= r, 1.0, 0.0).astype(bf16)
    upper = jnp.where(r <= c, 1.0, 0.0).astype(bf16)
    cc = _dot(lower, jnp.concatenate(_split3(lf_col), axis=1))
    cum_c = cc[:, 0:LANE] + cc[:, LANE:2 * LANE] + cc[:, 2 * LANE:3 * LANE] + carry_c[...]
    cumc_ref[...] = cum_c[:, 0:N_HEADS]
    carry_c[...] = cum_c[tm - 1:tm, :]
    cr = _dot(jnp.concatenate(_split3(lf_row), axis=0), upper)
    cum_r = cr[0:F_PAD] + cr[F_PAD:2 * F_PAD] + cr[2 * F_PAD:3 * F_PAD] + carry_r[:, 0:1]
    cumr_ref[...] = cum_r[0:N_HEADS]
    carry_r[...] = jnp.broadcast_to(cum_r[:, tm - 1:tm], carry_r.shape)

    u_ref[...] = _dot(xb, w_ref[:, C_U:C_G])
    sg_ref[...] = _sigmoid(_dot(xb, w_ref[:, C_G:C_END]))


def _proj_prompt(x, w_all, w_t, brow, bcol, batch, seq):
    m = batch * seq
    nt = seq // TM
    tok = lambda b, i: (b * nt + i, 0)
    tr = lambda b, i: (b, 0, i)
    out_shape = (
        jax.ShapeDtypeStruct((batch, D_ATTN, seq), bf16),
        jax.ShapeDtypeStruct((m, D_ATTN), f32),
        jax.ShapeDtypeStruct((m, D_ATTN), bf16),
        jax.ShapeDtypeStruct((m, D_ATTN), f32),
        jax.ShapeDtypeStruct((batch, D_ATTN, seq), bf16),
        jax.ShapeDtypeStruct((m, N_HEADS), f32),
        jax.ShapeDtypeStruct((m, N_HEADS), f32),
        jax.ShapeDtypeStruct((batch, N_HEADS, seq), f32),
        jax.ShapeDtypeStruct((m, D_POOL), f32),
        jax.ShapeDtypeStruct((m, 2 * D_MODEL), f32),
    )
    out_specs = (
        pl.BlockSpec((None, D_ATTN, TM), tr),
        pl.BlockSpec((TM, D_ATTN), tok),
        pl.BlockSpec((TM, D_ATTN), tok),
        pl.BlockSpec((TM, D_ATTN), tok),
        pl.BlockSpec((None, D_ATTN, TM), tr),
        pl.BlockSpec((TM, N_HEADS), tok),
        pl.BlockSpec((TM, N_HEADS), tok),
        pl.BlockSpec((None, N_HEADS, TM), tr),
        pl.BlockSpec((TM, D_POOL), tok),
        pl.BlockSpec((TM, 2 * D_MODEL), tok),
    )
    in_specs = [
        pl.BlockSpec((TM, D_MODEL), tok),
        _const_spec(w_all.shape),
        _const_spec(w_t.shape),
        _const_spec(brow.shape),
        _const_spec(bcol.shape),
    ]
    return pl.pallas_call(
        functools.partial(_proj_prompt_kernel, tm=TM),
        grid=(batch, nt),
        in_specs=in_specs,
        out_specs=out_specs,
        out_shape=out_shape,
        scratch_shapes=[pltpu.VMEM((1, LANE), f32), pltpu.VMEM((F_PAD, LANE), f32)],
        compiler_params=_params(("arbitrary", "arbitrary")),
        name="proj_prompt",
    )(x, w_all, w_t, brow, bcol)


def _attn_prompt_kernel(qT_ref, kb_ref, vT_ref, cumr_ref, cumc_ref, oT_ref, *, tq):
    i = pl.program_id(1)
    r = lax.broadcasted_iota(jnp.int32, (tq, tq), 0)
    c = lax.broadcasted_iota(jnp.int32, (tq, tq), 1)
    causal = r <= c
    zeros = jnp.zeros((HEAD_DIM, tq), bf16)

    for h in range(N_HEADS):
        pair = h // 2
        qh = qT_ref[h * HEAD_DIM:(h + 1) * HEAD_DIM, :]
        q2 = jnp.concatenate([qh, zeros] if h % 2 == 0 else [zeros, qh], axis=0)
        cq = cumr_ref[h:h + 1, :]

        def step(j, carry, masked, h=h, pair=pair, q2=q2, cq=cq):
            m, l, acc = carry
            off = pl.multiple_of(j * tq, tq)
            kblk = kb_ref[pl.ds(off, tq), pair * LANE:(pair + 1) * LANE]
            s = _dot(kblk, q2)
            ck = cumc_ref[pl.ds(off, tq), h:h + 1]
            s = s + (cq - ck)
            if masked:
                s = jnp.where(causal, s, NEG_INF)
            m_new = jnp.maximum(m, jnp.max(s, axis=0, keepdims=True))
            alpha = jnp.exp(m - m_new)
            p = jnp.exp(s - m_new)
            l = alpha * l + jnp.sum(p, axis=0, keepdims=True)
            vblk = vT_ref[h * HEAD_DIM:(h + 1) * HEAD_DIM, pl.ds(off, tq)]
            acc = alpha * acc + _dot(vblk, p.astype(bf16))
            return m_new, l, acc

        init = (jnp.full((1, tq), NEG_INF, f32), jnp.zeros((1, tq), f32), jnp.zeros((HEAD_DIM, tq), f32))
        carry = lax.fori_loop(0, i, functools.partial(step, masked=False), init)
        _, l, acc = step(i, carry, True)
        oT_ref[h * HEAD_DIM:(h + 1) * HEAD_DIM, :] = (acc / l).astype(bf16)


def _attn_prompt(qT, kb, vT, cum_r, cum_c, batch, seq):
    nq = seq // TQ
    return pl.pallas_call(
        functools.partial(_attn_prompt_kernel, tq=TQ),
        grid=(batch, nq),
        in_specs=[
            pl.BlockSpec((None, D_ATTN, TQ), lambda b, i: (b, 0, i)),
            pl.BlockSpec((seq, D_ATTN), lambda b, i: (b, 0)),
            pl.BlockSpec((None, D_ATTN, seq), lambda b, i: (b, 0, 0)),
            pl.BlockSpec((None, N_HEADS, TQ), lambda b, i: (b, 0, i)),
            pl.BlockSpec((seq, N_HEADS), lambda b, i: (b, 0)),
        ],
        out_specs=pl.BlockSpec((None, D_ATTN, TQ), lambda b, i: (b, 0, i)),
        out_shape=jax.ShapeDtypeStruct((batch, D_ATTN, seq), bf16),
        compiler_params=_params(("arbitrary", "arbitrary")),
        name="attn_prompt",
    )(qT, kb, vT, cum_r, cum_c)


def _pool_project(pooled, wpool_ref, scale_ref):
    outs = [_dot(pooled[g].astype(bf16), wpool_ref[g]) for g in range(len(POOL_WINDOWS))]
    return jnp.concatenate(outs, axis=-1) * scale_ref[...]


def _merge_tail(x, pa, ob, sg_ref, wpb_ref, wout_ref, g_ref, b_ref, alpha):
    pb = _dot(ob.astype(bf16), wpb_ref[...])
    merged = sg_ref[:, 0:D_MODEL] * pa + sg_ref[:, D_MODEL:2 * D_MODEL] * pb
    mix = _dot(merged.astype(bf16), wout_ref[...])
    return _layer_norm(alpha * x + mix, g_ref[...], b_ref[...])


def _merge_prompt_kernel(x_ref, oT_ref, u_ref, uh_ref, sg_ref, wpool_ref, scale_ref, wpa_ref, wpb_ref, wout_ref,
                         g_ref, b_ref, x1_ref, *, tm, alpha):
    i = pl.program_id(1)
    u = u_ref[...]
    halo = jnp.where(i > 0, uh_ref[...], 0.0)
    uext = jnp.concatenate([halo, u], axis=0)
    pos = i * tm + lax.broadcasted_iota(jnp.int32, (tm, 1), 0)
    pooled = []
    for g, w in enumerate(POOL_WINDOWS):
        sl = slice(g * POOL_GROUP, (g + 1) * POOL_GROUP)
        a = uext[:, sl]
        for st in range(g + 1):
            a = a + pltpu.roll(a, 2 ** st, 0)
        cnt = jnp.minimum(pos + 1, w).astype(f32)
        pooled.append(a[POOL_HALO:, :] / cnt - u[:, sl])
    ob = _pool_project(pooled, wpool_ref, scale_ref)
    pa = lax.dot_general(oT_ref[...], wpa_ref[...], TN_DIMS, preferred_element_type=f32)
    x1_ref[...] = _merge_tail(x_ref[...], pa, ob, sg_ref, wpb_ref, wout_ref, g_ref, b_ref, alpha)


def _merge_prompt(x, oT, u, sg, wpool, scale, wpa, wpb, wout, g, b, batch, seq, alpha):
    nt = seq // TM
    hb = TM // POOL_HALO
    tok = lambda bi, i: (bi * nt + i, 0)
    halo = lambda bi, i: (bi * (seq // POOL_HALO) + jnp.maximum(i * hb - 1, 0), 0)
    return pl.pallas_call(
        functools.partial(_merge_prompt_kernel, tm=TM, alpha=alpha),
        grid=(batch, nt),
        in_specs=[
            pl.BlockSpec((TM, D_MODEL), tok),
            pl.BlockSpec((None, D_ATTN, TM), lambda bi, i: (bi, 0, i)),
            pl.BlockSpec((TM, D_POOL), tok),
            pl.BlockSpec((POOL_HALO, D_POOL), halo),
            pl.BlockSpec((TM, 2 * D_MODEL), tok),
            _const_spec(wpool.shape), _const_spec(scale.shape), _const_spec(wpa.shape), _const_spec(wpb.shape),
            _const_spec(wout.shape), _const_spec(g.shape), _const_spec(b.shape),
        ],
        out_specs=pl.BlockSpec((TM, D_MODEL), tok),
        out_shape=jax.ShapeDtypeStruct((batch * seq, D_MODEL), f32),
        compiler_params=_params(("arbitrary", "arbitrary")),
        name="merge_prompt",
    )(x, oT, u, u, sg, wpool, scale, wpa, wpb, wout, g, b)


def _merge_sample_kernel(x_ref, oa_ref, u_ref, st_ref, sg_ref, wpool_ref, scale_ref, wpa_ref, wpb_ref, wout_ref,
                         g_ref, b_ref, x1_ref, *, alpha):
    u = u_ref[...]
    pooled = []
    for g, w in enumerate(POOL_WINDOWS):
        sl = slice(g * POOL_GROUP, (g + 1) * POOL_GROUP)
        a = u[:, sl]
        for t in range(POOL_STATE - (w - 1), POOL_STATE):
            a = a + st_ref[t, :, sl]
        pooled.append(a / float(w) - u[:, sl])
    ob = _pool_project(pooled, wpool_ref, scale_ref)
    pa = _dot(oa_ref[...].astype(bf16), wpa_ref[...])
    x1_ref[...] = _merge_tail(x_ref[...], pa, ob, sg_ref, wpb_ref, wout_ref, g_ref, b_ref, alpha)


def _merge_sample(x, oa, u, st, sg, wpool, scale, wpa, wpb, wout, g, b, alpha):
    args = (x, oa, u, st, sg, wpool, scale, wpa, wpb, wout, g, b)
    return pl.pallas_call(
        functools.partial(_merge_sample_kernel, alpha=alpha),
        grid=(1,),
        in_specs=[_const_spec(a.shape) for a in args],
        out_specs=_full_spec(x.shape),
        out_shape=jax.ShapeDtypeStruct(x.shape, f32),
        compiler_params=_params(("arbitrary",)),
        name="merge_sample",
    )(*args)


def _ffn_prompt_kernel(x1_ref, wup_ref, cw_ref, cb_ref, wdn_ref, g_ref, b_ref, x2_ref, cs_ref, prev_ref, act_ref,
                       *, tm, alpha):
    i = pl.program_id(1)

    @pl.when(i == 0)
    def _():
        prev_ref[...] = jnp.zeros_like(prev_ref)

    x1 = x1_ref[...]
    xb = x1.astype(bf16)
    row = lax.broadcasted_iota(jnp.int32, (tm, FF_CHUNK), 0)
    for c in range(D_FF // FF_CHUNK):
        hs = []
        for half in range(2):
            sl = slice(half * D_FF + c * FF_CHUNK, half * D_FF + (c + 1) * FF_CHUNK)
            up = _dot(xb, wup_ref[:, sl])
            p6 = prev_ref[6:7, sl]
            p7 = prev_ref[7:8, sl]
            s1 = jnp.where(row == 0, p7, pltpu.roll(up, 1, 0))
            s2 = jnp.where(row == 0, p6, jnp.where(row == 1, p7, pltpu.roll(up, 2, 0)))
            hs.append(cb_ref[:, sl] + up * cw_ref[2:3, sl] + s2 * cw_ref[0:1, sl] + s1 * cw_ref[1:2, sl])
            prev_ref[:, sl] = up[tm - 8:tm, :]
        act_ref[:, c * FF_CHUNK:(c + 1) * FF_CHUNK] = (_gelu_tanh(hs[0]) * hs[1]).astype(bf16)
    y = _dot(act_ref[...], wdn_ref[...])
    x2_ref[...] = _layer_norm(alpha * x1 + y, g_ref[...], b_ref[...])
    cs_ref[...] = prev_ref[...]


def _ffn_prompt(x1, wup, cw, cb, wdn, g, b, batch, seq, alpha):
    nt = seq // TM
    tok = lambda bi, i: (bi * nt + i, 0)
    return pl.pallas_call(
        functools.partial(_ffn_prompt_kernel, tm=TM, alpha=alpha),
        grid=(batch, nt),
        in_specs=[
            pl.BlockSpec((TM, D_MODEL), tok),
            _const_spec(wup.shape), _const_spec(cw.shape), _const_spec(cb.shape), _const_spec(wdn.shape),
            _const_spec(g.shape), _const_spec(b.shape),
        ],
        out_specs=(pl.BlockSpec((TM, D_MODEL), tok),
                   pl.BlockSpec((None, 8, 2 * D_FF), lambda bi, i: (bi, 0, 0))),
        out_shape=(jax.ShapeDtypeStruct((batch * seq, D_MODEL), f32),
                   jax.ShapeDtypeStruct((batch, 8, 2 * D_FF), f32)),
        scratch_shapes=[pltpu.VMEM((8, 2 * D_FF), f32), pltpu.VMEM((TM, D_FF), bf16)],
        compiler_params=_params(("arbitrary", "arbitrary")),
        name="ffn_prompt",
    )(x1, wup, cw, cb, wdn, g, b)


def _ffn_sample_kernel(x1_ref, s0_ref, s1_ref, wup_ref, cw_ref, cb_ref, wdn_ref, g_ref, b_ref, x2_ref, up_ref,
                       *, alpha):
    x1 = x1_ref[...]
    up = _dot(x1.astype(bf16), wup_ref[...])
    up_ref[...] = up
    h = cb_ref[...] + up * cw_ref[2:3, :] + s0_ref[...] * cw_ref[0:1, :] + s1_ref[...] * cw_ref[1:2, :]
    act = _gelu_tanh(h[:, 0:D_FF]) * h[:, D_FF:2 * D_FF]
    y = _dot(act.astype(bf16), wdn_ref[...])
    x2_ref[...] = _layer_norm(alpha * x1 + y, g_ref[...], b_ref[...])


def _ffn_sample(x1, s0, s1, wup, cw, cb, wdn, g, b, alpha):
    args = (x1, s0, s1, wup, cw, cb, wdn, g, b)
    n = x1.shape[0]
    return pl.pallas_call(
        functools.partial(_ffn_sample_kernel, alpha=alpha),
        grid=(1,),
        in_specs=[_const_spec(a.shape) for a in args],
        out_specs=(_full_spec(x1.shape), _full_spec((n, 2 * D_FF))),
        out_shape=(jax.ShapeDtypeStruct(x1.shape, f32), jax.ShapeDtypeStruct((n, 2 * D_FF), f32)),
        compiler_params=_params(("arbitrary",)),
        name="ffn_sample",
    )(*args)


def _proj_sample_kernel(x_ref, w_ref, brow_ref, q_ref, k_ref, v_ref, logf_ref, u_ref, sg_ref):
    z = _dot(x_ref[...].astype(bf16), w_ref[...])
    q_ref[...] = z[:, C_Q:C_K]
    k_ref[...] = z[:, C_K:C_V]
    v_ref[...] = z[:, C_V:C_F]
    logf_ref[...] = _log_sigmoid(z[:, C_F:C_U] + brow_ref[...])[:, 0:N_HEADS]
    u_ref[...] = z[:, C_U:C_G]
    sg_ref[...] = _sigmoid(z[:, C_G:C_END])


def _proj_sample(x, w_all, brow):
    n = x.shape[0]
    shapes = ((n, D_ATTN), (n, D_ATTN), (n, D_ATTN), (n, N_HEADS), (n, D_POOL), (n, 2 * D_MODEL))
    return pl.pallas_call(
        _proj_sample_kernel,
        grid=(1,),
        in_specs=[_const_spec(x.shape), _const_spec(w_all.shape), _const_spec(brow.shape)],
        out_specs=tuple(_full_spec(s) for s in shapes),
        out_shape=tuple(jax.ShapeDtypeStruct(s, f32) for s in shapes),
        compiler_params=_params(("arbitrary",)),
        name="proj_sample",
    )(x, w_all, brow)


def _attn_sample_kernel(pt_ref, q_ref, kn_ref, vn_ref, fn_ref, *rest, n_pages):
    k_refs = rest[0:n_pages]
    v_refs = rest[n_pages:2 * n_pages]
    f_refs = rest[2 * n_pages:3 * n_pages]
    o_ref = rest[3 * n_pages]

    lane_head = lax.broadcasted_iota(jnp.int32, (N_HEADS, D_ATTN), 1) // HEAD_DIM
    own = lane_head == lax.broadcasted_iota(jnp.int32, (N_HEADS, D_ATTN), 0)
    q = q_ref[...].astype(bf16).astype(f32)
    qbd = jnp.where(own, q, 0.0).astype(bf16)

    f_all = jnp.concatenate([f_refs[j][...] for j in range(n_pages)], axis=0)
    a = lax.broadcasted_iota(jnp.int32, (PAGE_SIZE, PAGE_SIZE), 0)
    s = lax.broadcasted_iota(jnp.int32, (PAGE_SIZE, PAGE_SIZE), 1)
    later = jnp.where(a > s, 1.0, 0.0).astype(bf16)
    hi, mid, lo = _split3(f_all)
    within = _dot(hi, later) + _dot(mid, later) + _dot(lo, later)
    running = fn_ref[...]
    logits = [None] * n_pages
    for j in reversed(range(n_pages)):
        kp = k_refs[j][...].astype(bf16)
        sc = lax.dot_general(qbd, kp, NT_DIMS, preferred_element_type=f32)
        logits[j] = sc + (within[j * N_HEADS:(j + 1) * N_HEADS, :] + running)
        running = running + jnp.sum(f_refs[j][...], axis=1, keepdims=True)

    kn = kn_ref[...].astype(bf16).astype(f32)
    s_new = jnp.sum(jnp.where(own, q * kn, 0.0), axis=1, keepdims=True)
    m = s_new
    for j in range(n_pages):
        m = jnp.maximum(m, jnp.max(logits[j], axis=1, keepdims=True))
    p_new = jnp.exp(s_new - m)
    l = p_new
    acc = p_new * vn_ref[...].astype(bf16).astype(f32)
    for j in range(n_pages):
        p = jnp.exp(logits[j] - m)
        l = l + jnp.sum(p, axis=1, keepdims=True)
        acc = acc + _dot(p.astype(bf16), v_refs[j][...].astype(bf16))
    o_ref[...] = jnp.sum(jnp.where(own, acc / l, 0.0), axis=0, keepdims=True)


def _attn_sample(page_table, q, kn, vn, fn, cache_k, cache_v, cache_ft, layer):
    n, n_pages = page_table.shape
    row = pl.BlockSpec((None, 1, D_ATTN), lambda b, pt: (b, 0, 0))
    kv_specs = [pl.BlockSpec((None, None, PAGE_SIZE, D_ATTN), lambda b, pt, j=j: (layer, pt[b, j], 0, 0))
                for j in range(n_pages)]
    f_specs = [pl.BlockSpec((None, None, N_HEADS, PAGE_SIZE), lambda b, pt, j=j: (layer, pt[b, j], 0, 0))
               for j in range(n_pages)]
    grid_spec = pltpu.PrefetchScalarGridSpec(
        num_scalar_prefetch=1,
        grid=(n,),
        in_specs=[row, row, row, pl.BlockSpec((None, N_HEADS, 1), lambda b, pt: (b, 0, 0))]
        + kv_specs + kv_specs + f_specs,
        out_specs=row,
    )
    return pl.pallas_call(
        functools.partial(_attn_sample_kernel, n_pages=n_pages),
        grid_spec=grid_spec,
        out_shape=jax.ShapeDtypeStruct((n, 1, D_ATTN), f32),
        compiler_params=_params(("arbitrary",)),
        name="attn_sample",
    )(page_table, q, kn, vn, fn, *([cache_k] * n_pages), *([cache_v] * n_pages), *([cache_ft] * n_pages))


def kernel(x_prompt, x_sample, cache_k, cache_v, cache_logf, state_pool, state_conv, page_table, w_in, b_forget,
           w_pool, pool_scale, w_proj_a, w_proj_b, w_out, ln1_g, ln1_b, w_up, conv_w, conv_b, w_down, ln2_g, ln2_b):
    depth = w_in.shape[0]
    bp, sp, _ = x_prompt.shape
    bs, ss, _ = x_sample.shape
    assert ss == 1 and sp % TM == 0 and sp % TQ == 0 and sp >= POOL_STATE
    n_pool = cache_k.shape[1]
    alpha = (2 * depth) ** 0.25

    wq = w_in[:, :, 0:D_ATTN] * (HEAD_DIM ** -0.5)
    wk = w_in[:, :, D_ATTN:2 * D_ATTN]
    wv = w_in[:, :, 2 * D_ATTN:3 * D_ATTN]
    o0 = 3 * D_ATTN
    wf = w_in[:, :, o0:o0 + N_HEADS]
    wrest = w_in[:, :, o0 + N_HEADS:]
    wf_pad = jnp.pad(wf, ((0, 0), (0, 0), (0, LANE - N_HEADS)))
    w_all = jnp.concatenate([wq, wk, wv, wf_pad, wrest], axis=2).astype(bf16)
    wf_t = jnp.pad(jnp.swapaxes(wf, 1, 2), ((0, 0), (0, F_PAD - N_HEADS), (0, 0)))
    w_t = jnp.concatenate([jnp.swapaxes(wq, 1, 2), jnp.swapaxes(wv, 1, 2), wf_t], axis=1).astype(bf16)
    brow = jnp.pad(b_forget, ((0, 0), (0, LANE - N_HEADS)))[:, None, :]
    bcol = jnp.pad(b_forget, ((0, 0), (0, F_PAD - N_HEADS)))[:, :, None]
    wpool_b = w_pool.astype(bf16)
    wpa_b, wpb_b, wout_b = w_proj_a.astype(bf16), w_proj_b.astype(bf16), w_out.astype(bf16)
    wup_b, wdn_b = w_up.astype(bf16), w_down.astype(bf16)
    row = lambda a: a[:, None, :]
    scale_r, g1, b1, g2, b2, cb_r = row(pool_scale), row(ln1_g), row(ln1_b), row(ln2_g), row(ln2_b), row(conv_b)

    ck4 = cache_k.reshape(depth, n_pool, PAGE_SIZE, D_ATTN)
    cv4 = cache_v.reshape(depth, n_pool, PAGE_SIZE, D_ATTN)
    cft = jnp.swapaxes(cache_logf, 2, 3)
    st_pool_t = jnp.swapaxes(state_pool, 1, 2)

    xp = x_prompt.reshape(bp * sp, D_MODEL)
    xs = x_sample.reshape(bs, D_MODEL)
    kp, vp, fp, pp, cp = [], [], [], [], []
    ksm, vsm, fsm, psm, csm = [], [], [], [], []
    for l in range(depth):
        qT, k, kb, v, vT, logf, cum_c, cum_r, u, sg = _proj_prompt(xp, w_all[l], w_t[l], brow[l], bcol[l], bp, sp)
        oT = _attn_prompt(qT, kb, vT, cum_r, cum_c, bp, sp)
        x1 = _merge_prompt(xp, oT, u, sg, wpool_b[l], scale_r[l], wpa_b[l], wpb_b[l], wout_b[l], g1[l], b1[l],
                           bp, sp, alpha)
        xp, cs = _ffn_prompt(x1, wup_b[l], conv_w[l], cb_r[l], wdn_b[l], g2[l], b2[l], bp, sp, alpha)
        kp.append(k.reshape(bp, sp, N_HEADS, HEAD_DIM))
        vp.append(v.reshape(bp, sp, N_HEADS, HEAD_DIM))
        fp.append(logf.reshape(bp, sp, N_HEADS))
        pp.append(u.reshape(bp, sp, D_POOL)[:, sp - POOL_STATE:])
        cp.append(cs[:, 8 - CONV_STATE:])

        q_s, k_s, v_s, f_s, u_s, sg_s = _proj_sample(xs, w_all[l], brow[l])
        oa = _attn_sample(page_table, q_s[:, None, :], k_s[:, None, :], v_s[:, None, :], f_s[:, :, None],
                          ck4, cv4, cft, l)
        x1s = _merge_sample(xs, oa.reshape(bs, D_ATTN), u_s, st_pool_t[l], sg_s, wpool_b[l], scale_r[l], wpa_b[l],
                            wpb_b[l], wout_b[l], g1[l], b1[l], alpha)
        xs, up_s = _ffn_sample(x1s, state_conv[l, :, 0], state_conv[l, :, 1], wup_b[l], conv_w[l], cb_r[l], wdn_b[l],
                               g2[l], b2[l], alpha)
        ksm.append(k_s.reshape(bs, 1, N_HEADS, HEAD_DIM))
        vsm.append(v_s.reshape(bs, 1, N_HEADS, HEAD_DIM))
        fsm.append(f_s.reshape(bs, 1, N_HEADS))
        psm.append(jnp.concatenate([state_pool[l][:, 1:], u_s[:, None, :]], axis=1))
        csm.append(jnp.concatenate([state_conv[l][:, 1:], up_s[:, None, :]], axis=1))

    return (xp.reshape(bp, sp, D_MODEL), xs.reshape(bs, 1, D_MODEL),
            jnp.stack(kp), jnp.stack(vp), jnp.stack(fp), jnp.stack(pp), jnp.stack(cp),
            jnp.stack(ksm), jnp.stack(vsm), jnp.stack(fsm), jnp.stack(psm), jnp.stack(csm))
```

```python
import functools

import numpy as np
import jax
import jax.numpy as jnp
from jax import lax
from jax.experimental import pallas as pl
from jax.experimental.pallas import tpu as pltpu

D_MODEL = 1024
N_HEADS = 8
HEAD_DIM = 64
D_ATTN = N_HEADS * HEAD_DIM
D_POOL = D_MODEL // 2
POOL_WINDOWS = (2, 4, 8, 16)
POOL_GROUP = D_POOL // len(POOL_WINDOWS)
POOL_STATE = max(POOL_WINDOWS) - 1
POOL_HALO = 16
D_FF = 2816
CONV_STATE = 2
PAGE_SIZE = 128
LN_EPS = 1e-5
NEG_INF = -1e30
LANE = 128
F_PAD = 16
AUG = LANE // N_HEADS

C_Q, C_K, C_V = 0, D_ATTN, 2 * D_ATTN
C_F = 3 * D_ATTN
C_U = C_F + LANE
C_G = C_U + D_POOL
C_END = C_G + 2 * D_MODEL
R_F = 3 * D_ATTN
R_END = R_F + F_PAD

TM = 512
TQ = 512
FF_CHUNK = 256
VMEM_LIMIT = 56 * 1024 * 1024

f32 = jnp.float32
bf16 = jnp.bfloat16
NT_DIMS = (((1,), (1,)), ((), ()))
TN_DIMS = (((0,), (0,)), ((), ()))


def _dot(a, b):
    return jnp.dot(a, b, preferred_element_type=f32)


def _log_sigmoid(x):
    return jnp.minimum(x, 0.0) - jnp.log1p(jnp.exp(-jnp.abs(x)))


def _sigmoid(x):
    return 1.0 / (1.0 + jnp.exp(-x))


def _gelu_tanh(x):
    return 0.5 * x * (1.0 + jnp.tanh(0.7978845608028654 * (x + 0.044715 * (x * x * x))))


def _layer_norm(y, g, b):
    mu = jnp.mean(y, axis=-1, keepdims=True)
    d = y - mu
    var = jnp.mean(d * d, axis=-1, keepdims=True)
    return d * lax.rsqrt(var + LN_EPS) * g + b


def _split3(x):
    hi = x.astype(bf16)
    r1 = x - hi.astype(f32)
    mid = r1.astype(bf16)
    lo = (r1 - mid.astype(f32)).astype(bf16)
    return hi, mid, lo


def _const_spec(shape):
    nd = len(shape)
    return pl.BlockSpec(shape, lambda *_: (0,) * nd, pipeline_mode=pl.Buffered(1))


def _full_spec(shape):
    nd = len(shape)
    return pl.BlockSpec(shape, lambda *_: (0,) * nd)


def _params(semantics):
    return pltpu.CompilerParams(dimension_semantics=semantics, vmem_limit_bytes=VMEM_LIMIT)


def _decay_placement():
    e_c = np.zeros((3 * LANE, LANE), np.float32)
    e_r = np.zeros((LANE, 3 * F_PAD), np.float32)
    ones_c = np.zeros((1, LANE), np.float32)
    ones_r = np.zeros((LANE, 1), np.float32)
    for h in range(N_HEADS):
        for p in range(3):
            e_c[p * LANE + h, h * AUG + p] = -1.0
            e_r[h * AUG + 3 + p, p * F_PAD + h] = 1.0
            ones_c[0, h * AUG + 3 + p] = 1.0
            ones_r[h * AUG + p, 0] = 1.0
    return jnp.asarray(e_c, bf16), jnp.asarray(ones_c), jnp.asarray(e_r, bf16), jnp.asarray(ones_r)


def _proj_prompt_kernel(x_ref, w_ref, wt_ref, brow_ref, bcol_ref, ec_ref, onec_ref, er_ref, oner_ref,
                        qT_ref, qaug_ref, kT_ref, kb_ref, kaug_ref, vT_ref, vTb_ref, logfT_ref, u_ref, sg_ref,
                        carry_c, carry_r, *, tm):
    i = pl.program_id(1)

    @pl.when(i == 0)
    def _():
        carry_c[...] = jnp.zeros_like(carry_c)
        carry_r[...] = jnp.zeros_like(carry_r)

    xb = x_ref[...].astype(bf16)
    zt = lax.dot_general(wt_ref[...], xb, NT_DIMS, preferred_element_type=f32)
    qT_ref[...] = zt[0:D_ATTN].astype(bf16)
    kt = zt[D_ATTN:2 * D_ATTN]
    kT_ref[...] = kt
    kb_ref[...] = kt.T.astype(bf16)
    vt = zt[2 * D_ATTN:R_F]
    vT_ref[...] = vt
    vTb_ref[...] = vt.astype(bf16)
    lf_row = _log_sigmoid(zt[R_F:R_END] + bcol_ref[...])
    logfT_ref[...] = lf_row[0:N_HEADS]
    lf_col = _log_sigmoid(_dot(xb, w_ref[:, C_F:C_U]) + brow_ref[...])

    r = lax.broadcasted_iota(jnp.int32, (tm, tm), 0)
    c = lax.broadcasted_iota(jnp.int32, (tm, tm), 1)
    lower = jnp.where(c <= r, 1.0, 0.0).astype(bf16)
    upper = jnp.where(r <= c, 1.0, 0.0).astype(bf16)
    cc = _dot(lower, jnp.concatenate(_split3(lf_col), axis=1))
    cum_c = cc[:, 0:LANE] + cc[:, LANE:2 * LANE] + cc[:, 2 * LANE:3 * LANE] + carry_c[...]
    carry_c[...] = cum_c[tm - 1:tm, :]
    kaug_ref[...] = (_dot(jnp.concatenate(_split3(cum_c), axis=1), ec_ref[...]) + onec_ref[...]).astype(bf16)
    cr = _dot(jnp.concatenate(_split3(lf_row), axis=0), upper)
    cum_r = cr[0:F_PAD] + cr[F_PAD:2 * F_PAD] + cr[2 * F_PAD:3 * F_PAD] + carry_r[:, 0:1]
    carry_r[...] = jnp.broadcast_to(cum_r[:, tm - 1:tm], carry_r.shape)
    qaug_ref[...] = (_dot(er_ref[...], jnp.concatenate(_split3(cum_r), axis=0)) + oner_ref[...]).astype(bf16)

    u_ref[...] = _dot(xb, w_ref[:, C_U:C_G])
    sg_ref[...] = _sigmoid(_dot(xb, w_ref[:, C_G:C_END])).astype(bf16)


def _proj_prompt(x, w_all, w_t, brow, bcol, placement, batch, seq):
    m = batch * seq
    nt = seq // TM
    tok = lambda b, i: (b * nt + i, 0)
    tr = lambda b, i: (b, 0, i)
    out_shape = (
        jax.ShapeDtypeStruct((batch, D_ATTN, seq), bf16),
        jax.ShapeDtypeStruct((batch, LANE, seq), bf16),
        jax.ShapeDtypeStruct((batch, D_ATTN, seq), f32),
        jax.ShapeDtypeStruct((m, D_ATTN), bf16),
        jax.ShapeDtypeStruct((m, LANE), bf16),
        jax.ShapeDtypeStruct((batch, D_ATTN, seq), f32),
        jax.ShapeDtypeStruct((batch, D_ATTN, seq), bf16),
        jax.ShapeDtypeStruct((batch, N_HEADS, seq), f32),
        jax.ShapeDtypeStruct((m, D_POOL), f32),
        jax.ShapeDtypeStruct((m, 2 * D_MODEL), bf16),
    )
    out_specs = (
        pl.BlockSpec((None, D_ATTN, TM), tr),
        pl.BlockSpec((None, LANE, TM), tr),
        pl.BlockSpec((None, D_ATTN, TM), tr),
        pl.BlockSpec((TM, D_ATTN), tok),
        pl.BlockSpec((TM, LANE), tok),
        pl.BlockSpec((None, D_ATTN, TM), tr),
        pl.BlockSpec((None, D_ATTN, TM), tr),
        pl.BlockSpec((None, N_HEADS, TM), tr),
        pl.BlockSpec((TM, D_POOL), tok),
        pl.BlockSpec((TM, 2 * D_MODEL), tok),
    )
    consts = (w_all, w_t, brow, bcol) + tuple(placement)
    return pl.pallas_call(
        functools.partial(_proj_prompt_kernel, tm=TM),
        grid=(batch, nt),
        in_specs=[pl.BlockSpec((TM, D_MODEL), tok)] + [_const_spec(a.shape) for a in consts],
        out_specs=out_specs,
        out_shape=out_shape,
        scratch_shapes=[pltpu.VMEM((1, LANE), f32), pltpu.VMEM((F_PAD, LANE), f32)],
        compiler_params=_params(("arbitrary", "arbitrary")),
        name="proj_prompt",
    )(x, *consts)


def _attn_prompt_kernel(qT_ref, qaug_ref, kb_ref, kaug_ref, vT_ref, oT_ref, *, tq):
    i = pl.program_id(1)
    r = lax.broadcasted_iota(jnp.int32, (tq, tq), 0)
    c = lax.broadcasted_iota(jnp.int32, (tq, tq), 1)
    causal = r <= c
    zeros = jnp.zeros((HEAD_DIM, tq), bf16)
    slot_head = lax.broadcasted_iota(jnp.int32, (LANE, tq), 0) // AUG
    qa = qaug_ref[...].astype(f32)

    for pair in range(N_HEADS // 2):
        rhs = []
        for e in range(2):
            h = 2 * pair + e
            qh = qT_ref[h * HEAD_DIM:(h + 1) * HEAD_DIM, :]
            aug = jnp.where(slot_head == h, qa, 0.0).astype(bf16)
            rhs.append(jnp.concatenate(([qh, zeros] if e == 0 else [zeros, qh]) + [aug], axis=0))

        def step(j, carry, masked, pair=pair, rhs=rhs):
            off = pl.multiple_of(j * tq, tq)
            lhs = jnp.concatenate([kb_ref[pl.ds(off, tq), pair * LANE:(pair + 1) * LANE],
                                   kaug_ref[pl.ds(off, tq), :]], axis=1)
            out = []
            for e in range(2):
                h = 2 * pair + e
                m, l, acc = carry[e]
                s = _dot(lhs, rhs[e])
                if masked:
                    s = jnp.where(causal, s, NEG_INF)
                m_new = jnp.maximum(m, jnp.max(s, axis=0, keepdims=True))
                alpha = jnp.exp(m - m_new)
                p = jnp.exp(s - m_new)
                l = alpha * l + jnp.sum(p, axis=0, keepdims=True)
                vblk = vT_ref[h * HEAD_DIM:(h + 1) * HEAD_DIM, pl.ds(off, tq)]
                acc = alpha * acc + _dot(vblk, p.astype(bf16))
                out.append((m_new, l, acc))
            return tuple(out)

        one = (jnp.full((1, tq), NEG_INF, f32), jnp.zeros((1, tq), f32), jnp.zeros((HEAD_DIM, tq), f32))
        carry = lax.fori_loop(0, i, functools.partial(step, masked=False), (one, one))
        carry = step(i, carry, True)
        for e in range(2):
            h = 2 * pair + e
            _, l, acc = carry[e]
            oT_ref[h * HEAD_DIM:(h + 1) * HEAD_DIM, :] = (acc / l).astype(bf16)


def _attn_prompt(qT, qaug, kb, kaug, vTb, batch, seq):
    nq = seq // TQ
    return pl.pallas_call(
        functools.partial(_attn_prompt_kernel, tq=TQ),
        grid=(batch, nq),
        in_specs=[
            pl.BlockSpec((None, D_ATTN, TQ), lambda b, i: (b, 0, i)),
            pl.BlockSpec((None, LANE, TQ), lambda b, i: (b, 0, i)),
            pl.BlockSpec((seq, D_ATTN), lambda b, i: (b, 0)),
            pl.BlockSpec((seq, LANE), lambda b, i: (b, 0)),
            pl.BlockSpec((None, D_ATTN, seq), lambda b, i: (b, 0, 0)),
        ],
        out_specs=pl.BlockSpec((None, D_ATTN, TQ), lambda b, i: (b, 0, i)),
        out_shape=jax.ShapeDtypeStruct((batch, D_ATTN, seq), bf16),
        compiler_params=_params(("arbitrary", "arbitrary")),
        name="attn_prompt",
    )(qT, qaug, kb, kaug, vTb)


def _pool_project(pooled, wpool_ref, scale_ref):
    outs = [_dot(pooled[g].astype(bf16), wpool_ref[g]) for g in range(len(POOL_WINDOWS))]
    return jnp.concatenate(outs, axis=-1) * scale_ref[...]


def _merge_tail(x, pa, ob, sg_ref, wpb_ref, wout_ref, g_ref, b_ref, alpha):
    pb = _dot(ob.astype(bf16), wpb_ref[...])
    merged = sg_ref[:, 0:D_MODEL] * pa + sg_ref[:, D_MODEL:2 * D_MODEL] * pb
    mix = _dot(merged.astype(bf16), wout_ref[...])
    return _layer_norm(alpha * x + mix, g_ref[...], b_ref[...])


def _merge_prompt_kernel(x_ref, oT_ref, u_ref, uh_ref, sg_ref, wpool_ref, scale_ref, wpa_ref, wpb_ref, wout_ref,
                         g_ref, b_ref, x1_ref, *, tm, alpha):
    i = pl.program_id(1)
    u = u_ref[...]
    halo = jnp.where(i > 0, uh_ref[...], 0.0)
    uext = jnp.concatenate([halo, u], axis=0)
    pos = i * tm + lax.broadcasted_iota(jnp.int32, (tm, 1), 0)
    pooled = []
    for g, w in enumerate(POOL_WINDOWS):
        sl = slice(g * POOL_GROUP, (g + 1) * POOL_GROUP)
        a = uext[:, sl]
        for st in range(g + 1):
            a = a + pltpu.roll(a, 2 ** st, 0)
        cnt = jnp.minimum(pos + 1, w).astype(f32)
        pooled.append(a[POOL_HALO:, :] / cnt - u[:, sl])
    ob = _pool_project(pooled, wpool_ref, scale_ref)
    pa = lax.dot_general(oT_ref[...], wpa_ref[...], TN_DIMS, preferred_element_type=f32)
    x1_ref[...] = _merge_tail(x_ref[...], pa, ob, sg_ref, wpb_ref, wout_ref, g_ref, b_ref, alpha)


def _merge_prompt(x, oT, u, sg, wpool, scale, wpa, wpb, wout, g, b, batch, seq, alpha):
    nt = seq // TM
    hb = TM // POOL_HALO
    tok = lambda bi, i: (bi * nt + i, 0)
    halo = lambda bi, i: (bi * (seq // POOL_HALO) + jnp.maximum(i * hb - 1, 0), 0)
    return pl.pallas_call(
        functools.partial(_merge_prompt_kernel, tm=TM, alpha=alpha),
        grid=(batch, nt),
        in_specs=[
            pl.BlockSpec((TM, D_MODEL), tok),
            pl.BlockSpec((None, D_ATTN, TM), lambda bi, i: (bi, 0, i)),
            pl.BlockSpec((TM, D_POOL), tok),
            pl.BlockSpec((POOL_HALO, D_POOL), halo),
            pl.BlockSpec((TM, 2 * D_MODEL), tok),
            _const_spec(wpool.shape), _const_spec(scale.shape), _const_spec(wpa.shape), _const_spec(wpb.shape),
            _const_spec(wout.shape), _const_spec(g.shape), _const_spec(b.shape),
        ],
        out_specs=pl.BlockSpec((TM, D_MODEL), tok),
        out_shape=jax.ShapeDtypeStruct((batch * seq, D_MODEL), f32),
        compiler_params=_params(("arbitrary", "arbitrary")),
        name="merge_prompt",
    )(x, oT, u, u, sg, wpool, scale, wpa, wpb, wout, g, b)


def _merge_sample_kernel(x_ref, oa_ref, u_ref, st_ref, sg_ref, wpool_ref, scale_ref, wpa_ref, wpb_ref, wout_ref,
                         g_ref, b_ref, x1_ref, *, alpha):
    u = u_ref[...]
    pooled = []
    for g, w in enumerate(POOL_WINDOWS):
        sl = slice(g * POOL_GROUP, (g + 1) * POOL_GROUP)
        a = u[:, sl]
        for t in range(POOL_STATE - (w - 1), POOL_STATE):
            a = a + st_ref[t, :, sl]
        pooled.append(a / float(w) - u[:, sl])
    ob = _pool_project(pooled, wpool_ref, scale_ref)
    pa = _dot(oa_ref[...].astype(bf16), wpa_ref[...])
    x1_ref[...] = _merge_tail(x_ref[...], pa, ob, sg_ref, wpb_ref, wout_ref, g_ref, b_ref, alpha)


def _merge_sample(x, oa, u, st, sg, wpool, scale, wpa, wpb, wout, g, b, alpha):
    args = (x, oa, u, st, sg, wpool, scale, wpa, wpb, wout, g, b)
    return pl.pallas_call(
        functools.partial(_merge_sample_kernel, alpha=alpha),
        grid=(1,),
        in_specs=[_const_spec(a.shape) for a in args],
        out_specs=_full_spec(x.shape),
        out_shape=jax.ShapeDtypeStruct(x.shape, f32),
        compiler_params=_params(("arbitrary",)),
        name="merge_sample",
    )(*args)


def _ffn_prompt_kernel(x1_ref, wup_ref, cw_ref, cb_ref, wdn_ref, g_ref, b_ref, x2_ref, cs_ref, prev_ref, act_ref,
                       *, tm, alpha):
    i = pl.program_id(1)

    @pl.when(i == 0)
    def _():
        prev_ref[...] = jnp.zeros_like(prev_ref)

    x1 = x1_ref[...]
    xb = x1.astype(bf16)
    row = lax.broadcasted_iota(jnp.int32, (tm, FF_CHUNK), 0)
    for c in range(D_FF // FF_CHUNK):
        hs = []
        for half in range(2):
            sl = slice(half * D_FF + c * FF_CHUNK, half * D_FF + (c + 1) * FF_CHUNK)
            up = _dot(xb, wup_ref[:, sl])
            p6 = prev_ref[6:7, sl]
            p7 = prev_ref[7:8, sl]
            s1 = jnp.where(row == 0, p7, pltpu.roll(up, 1, 0))
            s2 = jnp.where(row == 0, p6, jnp.where(row == 1, p7, pltpu.roll(up, 2, 0)))
            hs.append(cb_ref[:, sl] + up * cw_ref[2:3, sl] + s2 * cw_ref[0:1, sl] + s1 * cw_ref[1:2, sl])
            prev_ref[:, sl] = up[tm - 8:tm, :]
        act_ref[:, c * FF_CHUNK:(c + 1) * FF_CHUNK] = (_gelu_tanh(hs[0]) * hs[1]).astype(bf16)
    y = _dot(act_ref[...], wdn_ref[...])
    x2_ref[...] = _layer_norm(alpha * x1 + y, g_ref[...], b_ref[...])
    cs_ref[...] = prev_ref[...]


def _ffn_prompt(x1, wup, cw, cb, wdn, g, b, batch, seq, alpha):
    nt = seq // TM
    tok = lambda bi, i: (bi * nt + i, 0)
    return pl.pallas_call(
        functools.partial(_ffn_prompt_kernel, tm=TM, alpha=alpha),
        grid=(batch, nt),
        in_specs=[
            pl.BlockSpec((TM, D_MODEL), tok),
            _const_spec(wup.shape), _const_spec(cw.shape), _const_spec(cb.shape), _const_spec(wdn.shape),
            _const_spec(g.shape), _const_spec(b.shape),
        ],
        out_specs=(pl.BlockSpec((TM, D_MODEL), tok),
                   pl.BlockSpec((None, 8, 2 * D_FF), lambda bi, i: (bi, 0, 0))),
        out_shape=(jax.ShapeDtypeStruct((batch * seq, D_MODEL), f32),
                   jax.ShapeDtypeStruct((batch, 8, 2 * D_FF), f32)),
        scratch_shapes=[pltpu.VMEM((8, 2 * D_FF), f32), pltpu.VMEM((TM, D_FF), bf16)],
        compiler_params=_params(("arbitrary", "arbitrary")),
        name="ffn_prompt",
    )(x1, wup, cw, cb, wdn, g, b)


def _ffn_sample_kernel(x1_ref, s0_ref, s1_ref, wup_ref, cw_ref, cb_ref, wdn_ref, g_ref, b_ref, x2_ref, up_ref,
                       *, alpha):
    x1 = x1_ref[...]
    up = _dot(x1.astype(bf16), wup_ref[...])
    up_ref[...] = up
    h = cb_ref[...] + up * cw_ref[2:3, :] + s0_ref[...] * cw_ref[0:1, :] + s1_ref[...] * cw_ref[1:2, :]
    act = _gelu_tanh(h[:, 0:D_FF]) * h[:, D_FF:2 * D_FF]
    y = _dot(act.astype(bf16), wdn_ref[...])
    x2_ref[...] = _layer_norm(alpha * x1 + y, g_ref[...], b_ref[...])


def _ffn_sample(x1, s0, s1, wup, cw, cb, wdn, g, b, alpha):
    args = (x1, s0, s1, wup, cw, cb, wdn, g, b)
    n = x1.shape[0]
    return pl.pallas_call(
        functools.partial(_ffn_sample_kernel, alpha=alpha),
        grid=(1,),
        in_specs=[_const_spec(a.shape) for a in args],
        out_specs=(_full_spec(x1.shape), _full_spec((n, 2 * D_FF))),
        out_shape=(jax.ShapeDtypeStruct(x1.shape, f32), jax.ShapeDtypeStruct((n, 2 * D_FF), f32)),
        compiler_params=_params(("arbitrary",)),
        name="ffn_sample",
    )(*args)


def _proj_sample_kernel(x_ref, w_ref, brow_ref, q_ref, k_ref, v_ref, logf_ref, u_ref, sg_ref):
    z = _dot(x_ref[...].astype(bf16), w_ref[...])
    q_ref[...] = z[:, C_Q:C_K]
    k_ref[...] = z[:, C_K:C_V]
    v_ref[...] = z[:, C_V:C_F]
    logf_ref[...] = _log_sigmoid(z[:, C_F:C_U] + brow_ref[...])[:, 0:N_HEADS]
    u_ref[...] = z[:, C_U:C_G]
    sg_ref[...] = _sigmoid(z[:, C_G:C_END]).astype(bf16)


def _proj_sample(x, w_all, brow):
    n = x.shape[0]
    shapes = ((n, D_ATTN), (n, D_ATTN), (n, D_ATTN), (n, N_HEADS), (n, D_POOL), (n, 2 * D_MODEL))
    dtypes = (f32, f32, f32, f32, f32, bf16)
    return pl.pallas_call(
        _proj_sample_kernel,
        grid=(1,),
        in_specs=[_const_spec(x.shape), _const_spec(w_all.shape), _const_spec(brow.shape)],
        out_specs=tuple(_full_spec(s) for s in shapes),
        out_shape=tuple(jax.ShapeDtypeStruct(s, d) for s, d in zip(shapes, dtypes)),
        compiler_params=_params(("arbitrary",)),
        name="proj_sample",
    )(x, w_all, brow)


def _attn_sample_kernel(pt_ref, q_ref, kn_ref, vn_ref, fn_ref, *rest, n_pages):
    k_refs = rest[0:n_pages]
    v_refs = rest[n_pages:2 * n_pages]
    f_refs = rest[2 * n_pages:3 * n_pages]
    o_ref = rest[3 * n_pages]

    lane_head = lax.broadcasted_iota(jnp.int32, (N_HEADS, D_ATTN), 1) // HEAD_DIM
    own = lane_head == lax.broadcasted_iota(jnp.int32, (N_HEADS, D_ATTN), 0)
    q = q_ref[...].astype(bf16).astype(f32)
    qbd = jnp.where(own, q, 0.0).astype(bf16)

    f_all = jnp.concatenate([f_refs[j][...] for j in range(n_pages)], axis=0)
    a = lax.broadcasted_iota(jnp.int32, (PAGE_SIZE, PAGE_SIZE), 0)
    s = lax.broadcasted_iota(jnp.int32, (PAGE_SIZE, PAGE_SIZE), 1)
    later = jnp.where(a > s, 1.0, 0.0).astype(bf16)
    hi, mid, lo = _split3(f_all)
    within = _dot(hi, later) + _dot(mid, later) + _dot(lo, later)
    running = fn_ref[...]
    logits = [None] * n_pages
    for j in reversed(range(n_pages)):
        sc = _dot(qbd, k_refs[j][...].astype(bf16))
        logits[j] = sc + (within[j * N_HEADS:(j + 1) * N_HEADS, :] + running)
        running = running + jnp.sum(f_refs[j][...], axis=1, keepdims=True)

    kn = kn_ref[...].astype(bf16).astype(f32)
    s_new = jnp.sum(jnp.where(own, q * kn, 0.0), axis=1, keepdims=True)
    m = s_new
    for j in range(n_pages):
        m = jnp.maximum(m, jnp.max(logits[j], axis=1, keepdims=True))
    p_new = jnp.exp(s_new - m)
    l = p_new
    acc = p_new * vn_ref[...].astype(bf16).astype(f32)
    for j in range(n_pages):
        p = jnp.exp(logits[j] - m)
        l = l + jnp.sum(p, axis=1, keepdims=True)
        acc = acc + lax.dot_general(p.astype(bf16), v_refs[j][...].astype(bf16), NT_DIMS,
                                    preferred_element_type=f32)
    o_ref[...] = jnp.sum(jnp.where(own, acc / l, 0.0), axis=0, keepdims=True)


def _attn_sample(page_table, q, kn, vn, fn, cache_kt, cache_vt, cache_ft, layer):
    n, n_pages = page_table.shape
    row = pl.BlockSpec((None, 1, D_ATTN), lambda b, pt: (b, 0, 0))
    kv_specs = [pl.BlockSpec((None, None, D_ATTN, PAGE_SIZE), lambda b, pt, j=j: (layer, pt[b, j], 0, 0))
                for j in range(n_pages)]
    f_specs = [pl.BlockSpec((None, None, N_HEADS, PAGE_SIZE), lambda b, pt, j=j: (layer, pt[b, j], 0, 0))
               for j in range(n_pages)]
    grid_spec = pltpu.PrefetchScalarGridSpec(
        num_scalar_prefetch=1,
        grid=(n,),
        in_specs=[row, row, row, pl.BlockSpec((None, N_HEADS, 1), lambda b, pt: (b, 0, 0))]
        + kv_specs + kv_specs + f_specs,
        out_specs=row,
    )
    return pl.pallas_call(
        functools.partial(_attn_sample_kernel, n_pages=n_pages),
        grid_spec=grid_spec,
        out_shape=jax.ShapeDtypeStruct((n, 1, D_ATTN), f32),
        compiler_params=_params(("arbitrary",)),
        name="attn_sample",
    )(page_table, q, kn, vn, fn, *([cache_kt] * n_pages), *([cache_vt] * n_pages), *([cache_ft] * n_pages))


def kernel(x_prompt, x_sample, cache_k, cache_v, cache_logf, state_pool, state_conv, page_table, w_in, b_forget,
           w_pool, pool_scale, w_proj_a, w_proj_b, w_out, ln1_g, ln1_b, w_up, conv_w, conv_b, w_down, ln2_g, ln2_b):
    depth = w_in.shape[0]
    bp, sp, _ = x_prompt.shape
    bs, ss, _ = x_sample.shape
    assert ss == 1 and sp % TM == 0 and sp % TQ == 0 and sp >= POOL_STATE
    n_pool = cache_k.shape[1]
    alpha = (2 * depth) ** 0.25

    wq = w_in[:, :, 0:D_ATTN] * (HEAD_DIM ** -0.5)
    wk = w_in[:, :, D_ATTN:2 * D_ATTN]
    wv = w_in[:, :, 2 * D_ATTN:3 * D_ATTN]
    o0 = 3 * D_ATTN
    wf = w_in[:, :, o0:o0 + N_HEADS]
    wrest = w_in[:, :, o0 + N_HEADS:]
    wf_pad = jnp.pad(wf, ((0, 0), (0, 0), (0, LANE - N_HEADS)))
    w_all = jnp.concatenate([wq, wk, wv, wf_pad, wrest], axis=2).astype(bf16)
    tr = lambda a: jnp.swapaxes(a, 1, 2)
    wf_t = jnp.pad(tr(wf), ((0, 0), (0, F_PAD - N_HEADS), (0, 0)))
    w_t = jnp.concatenate([tr(wq), tr(wk), tr(wv), wf_t], axis=1).astype(bf16)
    brow = jnp.pad(b_forget, ((0, 0), (0, LANE - N_HEADS)))[:, None, :]
    bcol = jnp.pad(b_forget, ((0, 0), (0, F_PAD - N_HEADS)))[:, :, None]
    placement = _decay_placement()
    wpool_b = w_pool.astype(bf16)
    wpa_b, wpb_b, wout_b = w_proj_a.astype(bf16), w_proj_b.astype(bf16), w_out.astype(bf16)
    wup_b, wdn_b = w_up.astype(bf16), w_down.astype(bf16)
    row = lambda a: a[:, None, :]
    scale_r, g1, b1, g2, b2, cb_r = row(pool_scale), row(ln1_g), row(ln1_b), row(ln2_g), row(ln2_b), row(conv_b)

    ckt = jnp.transpose(cache_k, (0, 1, 3, 4, 2)).reshape(depth, n_pool, D_ATTN, PAGE_SIZE)
    cvt = jnp.transpose(cache_v, (0, 1, 3, 4, 2)).reshape(depth, n_pool, D_ATTN, PAGE_SIZE)
    cft = jnp.swapaxes(cache_logf, 2, 3)
    st_pool_t = jnp.swapaxes(state_pool, 1, 2)

    xp = x_prompt.reshape(bp * sp, D_MODEL)
    xs = x_sample.reshape(bs, D_MODEL)
    kp, vp, fp, pp, cp = [], [], [], [], []
    ksm, vsm, fsm, psm, csm = [], [], [], [], []
    for l in range(depth):
        qT, qaug, kT, kb, kaug, vT, vTb, logfT, u, sg = _proj_prompt(xp, w_all[l], w_t[l], brow[l], bcol[l],
                                                                     placement, bp, sp)
        oT = _attn_prompt(qT, qaug, kb, kaug, vTb, bp, sp)
        x1 = _merge_prompt(xp, oT, u, sg, wpool_b[l], scale_r[l], wpa_b[l], wpb_b[l], wout_b[l], g1[l], b1[l],
                           bp, sp, alpha)
        xp, cs = _ffn_prompt(x1, wup_b[l], conv_w[l], cb_r[l], wdn_b[l], g2[l], b2[l], bp, sp, alpha)
        kp.append(kT)
        vp.append(vT)
        fp.append(logfT)
        pp.append(u.reshape(bp, sp, D_POOL)[:, sp - POOL_STATE:])
        cp.append(cs[:, 8 - CONV_STATE:])

        q_s, k_s, v_s, f_s, u_s, sg_s = _proj_sample(xs, w_all[l], brow[l])
        oa = _attn_sample(page_table, q_s[:, None, :], k_s[:, None, :], v_s[:, None, :], f_s[:, :, None],
                          ckt, cvt, cft, l)
        x1s = _merge_sample(xs, oa.reshape(bs, D_ATTN), u_s, st_pool_t[l], sg_s, wpool_b[l], scale_r[l], wpa_b[l],
                            wpb_b[l], wout_b[l], g1[l], b1[l], alpha)
        xs, up_s = _ffn_sample(x1s, state_conv[l, :, 0], state_conv[l, :, 1], wup_b[l], conv_w[l], cb_r[l], wdn_b[l],
                               g2[l], b2[l], alpha)
        ksm.append(k_s.reshape(bs, 1, N_HEADS, HEAD_DIM))
        vsm.append(v_s.reshape(bs, 1, N_HEADS, HEAD_DIM))
        fsm.append(f_s.reshape(bs, 1, N_HEADS))
        psm.append(jnp.concatenate([state_pool[l][:, 1:], u_s[:, None, :]], axis=1))
        csm.append(jnp.concatenate([state_conv[l][:, 1:], up_s[:, None, :]], axis=1))

    heads_last = lambda t: jnp.transpose(jnp.stack(t).reshape(depth, bp, N_HEADS, HEAD_DIM, sp), (0, 1, 4, 2, 3))
    return (xp.reshape(bp, sp, D_MODEL), xs.reshape(bs, 1, D_MODEL),
            heads_last(kp), heads_last(vp), jnp.transpose(jnp.stack(fp), (0, 1, 3, 2)), jnp.stack(pp), jnp.stack(cp),
            jnp.stack(ksm), jnp.stack(vsm), jnp.stack(fsm), jnp.stack(psm), jnp.stack(csm))
```

```python
import functools
from typing import NamedTuple

import numpy as np
import jax
import jax.numpy as jnp
from jax import lax
from jax.experimental import pallas as pl
from jax.experimental.pallas import tpu as pltpu

D_MODEL = 1024
N_HEADS = 8
HEAD_DIM = 64
D_ATTN = N_HEADS * HEAD_DIM
D_POOL = D_MODEL // 2
POOL_WINDOWS = (2, 4, 8, 16)
POOL_GROUP = D_POOL // len(POOL_WINDOWS)
POOL_STATE = max(POOL_WINDOWS) - 1
POOL_HALO = 16
D_FF = 2816
CONV_STATE = 2
PAGE_SIZE = 128
LN_EPS = 1e-5
NEG_INF = -1e30
LANE = 128
F_PAD = 16
AUG = LANE // N_HEADS
V_PAD = 16
QK_AHEAD = 3
LOG2E = 1.4426950408889634

C_Q, C_K, C_V = 0, D_ATTN, 2 * D_ATTN
C_F = 3 * D_ATTN
C_U = C_F + LANE
C_G = C_U + D_POOL
C_END = C_G + 2 * D_MODEL
R_F = 3 * D_ATTN
R_END = R_F + F_PAD

TM = 512
TQ = 512
FF_CHUNK = 256
VMEM_LIMIT = 56 * 1024 * 1024

f32 = jnp.float32
bf16 = jnp.bfloat16
NT_DIMS = (((1,), (1,)), ((), ()))
TN_DIMS = (((0,), (0,)), ((), ()))


def _dot(a, b):
    return jnp.dot(a, b, preferred_element_type=f32)


def _log_sigmoid(x):
    return jnp.minimum(x, 0.0) - jnp.log1p(jnp.exp(-jnp.abs(x)))


def _sigmoid(x):
    return 1.0 / (1.0 + jnp.exp(-x))


def _gelu_tanh(x):
    return 0.5 * x * (1.0 + jnp.tanh(0.7978845608028654 * (x + 0.044715 * (x * x * x))))


def _layer_norm(y, g, b):
    mu = jnp.mean(y, axis=-1, keepdims=True)
    d = y - mu
    var = jnp.mean(d * d, axis=-1, keepdims=True)
    return d * lax.rsqrt(var + LN_EPS) * g + b


def _split3(x):
    hi = x.astype(bf16)
    r1 = x - hi.astype(f32)
    mid = r1.astype(bf16)
    lo = (r1 - mid.astype(f32)).astype(bf16)
    return hi, mid, lo


class _Layered(NamedTuple):
    array: jax.Array
    layer: int


def _operand(a):
    return a.array if isinstance(a, _Layered) else a


def _const_spec(a):
    if isinstance(a, _Layered):
        shape, layer = a.array.shape[1:], a.layer
        return pl.BlockSpec((None,) + shape, lambda *_: (layer,) + (0,) * len(shape), pipeline_mode=pl.Buffered(1))
    nd = len(a.shape)
    return pl.BlockSpec(a.shape, lambda *_: (0,) * nd, pipeline_mode=pl.Buffered(1))


def _full_spec(shape):
    nd = len(shape)
    return pl.BlockSpec(shape, lambda *_: (0,) * nd)


def _params(semantics):
    return pltpu.CompilerParams(dimension_semantics=semantics, vmem_limit_bytes=VMEM_LIMIT)


def _decay_placement():
    e_c = np.zeros((3 * LANE, LANE), np.float32)
    e_r = np.zeros((LANE, 3 * F_PAD), np.float32)
    ones_c = np.zeros((1, LANE), np.float32)
    ones_r = np.zeros((LANE, 1), np.float32)
    for h in range(N_HEADS):
        for p in range(3):
            e_c[p * LANE + h, h * AUG + p] = -1.0
            e_r[h * AUG + 3 + p, p * F_PAD + h] = 1.0
            ones_c[0, h * AUG + 3 + p] = 1.0
            ones_r[h * AUG + p, 0] = 1.0
    return jnp.asarray(e_c, bf16), jnp.asarray(ones_c), jnp.asarray(e_r, bf16), jnp.asarray(ones_r)


def _proj_prompt_kernel(x_ref, w_ref, wt_ref, brow_ref, bcol_ref, ec_ref, onec_ref, er_ref, oner_ref,
                        qT_ref, qaug_ref, kT_ref, kb_ref, kaug_ref, vT_ref, vTb_ref, logfT_ref, u_ref, sg_ref,
                        carry_c, carry_r, *, tm):
    i = pl.program_id(1)

    @pl.when(i == 0)
    def _():
        carry_c[...] = jnp.zeros_like(carry_c)
        carry_r[...] = jnp.zeros_like(carry_r)

    xb = x_ref[...].astype(bf16)
    zt = lax.dot_general(wt_ref[...], xb, NT_DIMS, preferred_element_type=f32)
    qT_ref[...] = zt[0:D_ATTN].astype(bf16)
    kt = zt[D_ATTN:2 * D_ATTN]
    kT_ref[...] = kt
    kb_ref[...] = kt.T.astype(bf16)
    vt = zt[2 * D_ATTN:R_F]
    vT_ref[...] = vt
    vTb_ref[...] = vt.astype(bf16)
    lf_row = _log_sigmoid(zt[R_F:R_END] + bcol_ref[...])
    logfT_ref[...] = lf_row[0:N_HEADS]
    lf_col = _log_sigmoid(_dot(xb, w_ref[:, C_F:C_U]) + brow_ref[...])

    r = lax.broadcasted_iota(jnp.int32, (tm, tm), 0)
    c = lax.broadcasted_iota(jnp.int32, (tm, tm), 1)
    lower = jnp.where(c <= r, 1.0, 0.0).astype(bf16)
    upper = jnp.where(r <= c, 1.0, 0.0).astype(bf16)
    cc = _dot(lower, jnp.concatenate(_split3(lf_col), axis=1))
    cum_c = cc[:, 0:LANE] + cc[:, LANE:2 * LANE] + cc[:, 2 * LANE:3 * LANE] + carry_c[...]
    carry_c[...] = cum_c[tm - 1:tm, :]
    kaug_ref[...] = (_dot(jnp.concatenate(_split3(cum_c * LOG2E), axis=1), ec_ref[...]) + onec_ref[...]).astype(bf16)
    cr = _dot(jnp.concatenate(_split3(lf_row), axis=0), upper)
    cum_r = cr[0:F_PAD] + cr[F_PAD:2 * F_PAD] + cr[2 * F_PAD:3 * F_PAD] + carry_r[:, 0:1]
    carry_r[...] = jnp.broadcast_to(cum_r[:, tm - 1:tm], carry_r.shape)
    qaug_ref[...] = (_dot(er_ref[...], jnp.concatenate(_split3(cum_r * LOG2E), axis=0)) + oner_ref[...]).astype(bf16)

    u_ref[...] = _dot(xb, w_ref[:, C_U:C_G])
    sg_ref[...] = _sigmoid(_dot(xb, w_ref[:, C_G:C_END])).astype(bf16)


def _proj_prompt(x, w_all, w_t, brow, bcol, placement, batch, seq):
    m = batch * seq
    nt = seq // TM
    tok = lambda b, i: (b * nt + i, 0)
    tr = lambda b, i: (b, 0, i)
    out_shape = (
        jax.ShapeDtypeStruct((batch, D_ATTN, seq), bf16),
        jax.ShapeDtypeStruct((batch, LANE, seq), bf16),
        jax.ShapeDtypeStruct((batch, D_ATTN, seq), f32),
        jax.ShapeDtypeStruct((m, D_ATTN), bf16),
        jax.ShapeDtypeStruct((m, LANE), bf16),
        jax.ShapeDtypeStruct((batch, D_ATTN, seq), f32),
        jax.ShapeDtypeStruct((batch, D_ATTN, seq), bf16),
        jax.ShapeDtypeStruct((batch, N_HEADS, seq), f32),
        jax.ShapeDtypeStruct((m, D_POOL), f32),
        jax.ShapeDtypeStruct((m, 2 * D_MODEL), bf16),
    )
    out_specs = (
        pl.BlockSpec((None, D_ATTN, TM), tr),
        pl.BlockSpec((None, LANE, TM), tr),
        pl.BlockSpec((None, D_ATTN, TM), tr),
        pl.BlockSpec((TM, D_ATTN), tok),
        pl.BlockSpec((TM, LANE), tok),
        pl.BlockSpec((None, D_ATTN, TM), tr),
        pl.BlockSpec((None, D_ATTN, TM), tr),
        pl.BlockSpec((None, N_HEADS, TM), tr),
        pl.BlockSpec((TM, D_POOL), tok),
        pl.BlockSpec((TM, 2 * D_MODEL), tok),
    )
    consts = (w_all, w_t, brow, bcol) + tuple(placement)
    return pl.pallas_call(
        functools.partial(_proj_prompt_kernel, tm=TM),
        grid=(batch, nt),
        in_specs=[pl.BlockSpec((TM, D_MODEL), tok)] + [_const_spec(a) for a in consts],
        out_specs=out_specs,
        out_shape=out_shape,
        scratch_shapes=[pltpu.VMEM((1, LANE), f32), pltpu.VMEM((F_PAD, LANE), f32)],
        compiler_params=_params(("arbitrary", "arbitrary")),
        name="proj_prompt",
    )(x, *map(_operand, consts))


def _attn_prompt_kernel(qT_ref, qaug_ref, kb_ref, kaug_ref, vT_ref, oT_ref, m_ref, acc_ref, *, tq):
    i = pl.program_id(1)
    r = lax.broadcasted_iota(jnp.int32, (tq, tq), 0)
    c = lax.broadcasted_iota(jnp.int32, (tq, tq), 1)
    causal = r <= c
    zeros = jnp.zeros((HEAD_DIM, tq), bf16)
    ones = jnp.ones((V_PAD, tq), bf16)
    slot_head = lax.broadcasted_iota(jnp.int32, (LANE, tq), 0) // AUG
    qa = qaug_ref[...].astype(f32)

    rhs = []
    for h in range(N_HEADS):
        qh = qT_ref[h * HEAD_DIM:(h + 1) * HEAD_DIM, :]
        aug = jnp.where(slot_head == h, qa, 0.0).astype(bf16)
        rhs.append(jnp.concatenate(([qh, zeros] if h % 2 == 0 else [zeros, qh]) + [aug], axis=0))
        m_ref[h] = jnp.full((1, tq), NEG_INF, f32)
        acc_ref[h] = jnp.zeros((HEAD_DIM + V_PAD, tq), f32)

    def step(j, masked):
        off = pl.multiple_of(j * tq, tq)
        ka = kaug_ref[pl.ds(off, tq), :]

        def scores(h):
            pair = h // 2
            lhs = jnp.concatenate([kb_ref[pl.ds(off, tq), pair * LANE:(pair + 1) * LANE], ka], axis=1)
            return _dot(lhs, rhs[h])

        ss = {h: scores(h) for h in range(QK_AHEAD)}
        for h in range(N_HEADS):
            if h + QK_AHEAD < N_HEADS:
                ss[h + QK_AHEAD] = scores(h + QK_AHEAD)
            s = ss.pop(h)
            if masked:
                s = jnp.where(causal, s, NEG_INF)
            m = m_ref[h]
            m_new = jnp.maximum(m, jnp.max(s, axis=0, keepdims=True))
            alpha = jnp.exp2(m - m_new)
            p = jnp.exp2(s - m_new).astype(bf16)
            va = jnp.concatenate([vT_ref[h * HEAD_DIM:(h + 1) * HEAD_DIM, pl.ds(off, tq)], ones], axis=0)
            acc_ref[h] = alpha * acc_ref[h] + _dot(va, p)
            m_ref[h] = m_new

    def body(j, carry):
        step(j, False)
        return carry

    lax.fori_loop(0, i, body, 0)
    step(i, True)
    for h in range(N_HEADS):
        acc = acc_ref[h]
        oT_ref[h * HEAD_DIM:(h + 1) * HEAD_DIM, :] = (acc[0:HEAD_DIM] / acc[HEAD_DIM:HEAD_DIM + 1]).astype(bf16)


def _attn_prompt(qT, qaug, kb, kaug, vTb, batch, seq):
    nq = seq // TQ
    return pl.pallas_call(
        functools.partial(_attn_prompt_kernel, tq=TQ),
        grid=(batch, nq),
        in_specs=[
            pl.BlockSpec((None, D_ATTN, TQ), lambda b, i: (b, 0, i)),
            pl.BlockSpec((None, LANE, TQ), lambda b, i: (b, 0, i)),
            pl.BlockSpec((seq, D_ATTN), lambda b, i: (b, 0)),
            pl.BlockSpec((seq, LANE), lambda b, i: (b, 0)),
            pl.BlockSpec((None, D_ATTN, seq), lambda b, i: (b, 0, 0)),
        ],
        out_specs=pl.BlockSpec((None, D_ATTN, TQ), lambda b, i: (b, 0, i)),
        out_shape=jax.ShapeDtypeStruct((batch, D_ATTN, seq), bf16),
        scratch_shapes=[pltpu.VMEM((N_HEADS, 1, TQ), f32),
                        pltpu.VMEM((N_HEADS, HEAD_DIM + V_PAD, TQ), f32)],
        compiler_params=_params(("arbitrary", "arbitrary")),
        name="attn_prompt",
    )(qT, qaug, kb, kaug, vTb)


def _pool_project(pooled, wpool_ref, scale_ref):
    outs = [_dot(pooled[g].astype(bf16), wpool_ref[g]) for g in range(len(POOL_WINDOWS))]
    return jnp.concatenate(outs, axis=-1) * scale_ref[...]


def _merge_tail(x, pa, ob, sg_ref, wpb_ref, wout_ref, g_ref, b_ref, alpha):
    pb = _dot(ob.astype(bf16), wpb_ref[...])
    merged = sg_ref[:, 0:D_MODEL] * pa + sg_ref[:, D_MODEL:2 * D_MODEL] * pb
    mix = _dot(merged.astype(bf16), wout_ref[...])
    return _layer_norm(alpha * x + mix, g_ref[...], b_ref[...])


def _merge_prompt_kernel(x_ref, oT_ref, u_ref, uh_ref, sg_ref, wpool_ref, scale_ref, wpa_ref, wpb_ref, wout_ref,
                         g_ref, b_ref, x1_ref, *, tm, alpha):
    i = pl.program_id(1)
    u = u_ref[...]
    halo = jnp.where(i > 0, uh_ref[...], 0.0)
    uext = jnp.concatenate([halo, u], axis=0)
    pos = i * tm + lax.broadcasted_iota(jnp.int32, (tm, 1), 0)
    pooled = []
    for g, w in enumerate(POOL_WINDOWS):
        sl = slice(g * POOL_GROUP, (g + 1) * POOL_GROUP)
        a = uext[:, sl]
        for st in range(g + 1):
            a = a + pltpu.roll(a, 2 ** st, 0)
        cnt = jnp.minimum(pos + 1, w).astype(f32)
        pooled.append(a[POOL_HALO:, :] / cnt - u[:, sl])
    ob = _pool_project(pooled, wpool_ref, scale_ref)
    pa = lax.dot_general(oT_ref[...], wpa_ref[...], TN_DIMS, preferred_element_type=f32)
    x1_ref[...] = _merge_tail(x_ref[...], pa, ob, sg_ref, wpb_ref, wout_ref, g_ref, b_ref, alpha)


def _merge_prompt(x, oT, u, sg, wpool, scale, wpa, wpb, wout, g, b, batch, seq, alpha):
    nt = seq // TM
    hb = TM // POOL_HALO
    tok = lambda bi, i: (bi * nt + i, 0)
    halo = lambda bi, i: (bi * (seq // POOL_HALO) + jnp.maximum(i * hb - 1, 0), 0)
    consts = (wpool, scale, wpa, wpb, wout, g, b)
    return pl.pallas_call(
        functools.partial(_merge_prompt_kernel, tm=TM, alpha=alpha),
        grid=(batch, nt),
        in_specs=[
            pl.BlockSpec((TM, D_MODEL), tok),
            pl.BlockSpec((None, D_ATTN, TM), lambda bi, i: (bi, 0, i)),
            pl.BlockSpec((TM, D_POOL), tok),
            pl.BlockSpec((POOL_HALO, D_POOL), halo),
            pl.BlockSpec((TM, 2 * D_MODEL), tok),
        ] + [_const_spec(a) for a in consts],
        out_specs=pl.BlockSpec((TM, D_MODEL), tok),
        out_shape=jax.ShapeDtypeStruct((batch * seq, D_MODEL), f32),
        compiler_params=_params(("arbitrary", "arbitrary")),
        name="merge_prompt",
    )(x, oT, u, u, sg, *map(_operand, consts))


def _merge_sample_kernel(x_ref, oa_ref, u_ref, st_ref, sg_ref, wpool_ref, scale_ref, wpa_ref, wpb_ref, wout_ref,
                         g_ref, b_ref, x1_ref, *, alpha):
    u = u_ref[...]
    pooled = []
    for g, w in enumerate(POOL_WINDOWS):
        sl = slice(g * POOL_GROUP, (g + 1) * POOL_GROUP)
        a = u[:, sl]
        for t in range(POOL_STATE - (w - 1), POOL_STATE):
            a = a + st_ref[t, :, sl]
        pooled.append(a / float(w) - u[:, sl])
    ob = _pool_project(pooled, wpool_ref, scale_ref)
    pa = _dot(oa_ref[...].astype(bf16), wpa_ref[...])
    x1_ref[...] = _merge_tail(x_ref[...], pa, ob, sg_ref, wpb_ref, wout_ref, g_ref, b_ref, alpha)


def _merge_sample(x, oa, u, st, sg, wpool, scale, wpa, wpb, wout, g, b, alpha):
    args = (x, oa, u, st, sg, wpool, scale, wpa, wpb, wout, g, b)
    return pl.pallas_call(
        functools.partial(_merge_sample_kernel, alpha=alpha),
        grid=(1,),
        in_specs=[_const_spec(a) for a in args],
        out_specs=_full_spec(x.shape),
        out_shape=jax.ShapeDtypeStruct(x.shape, f32),
        compiler_params=_params(("arbitrary",)),
        name="merge_sample",
    )(*map(_operand, args))


def _ffn_prompt_kernel(x1_ref, wup_ref, cw_ref, cb_ref, wdn_ref, g_ref, b_ref, x2_ref, cs_ref, prev_ref, act_ref,
                       *, tm, alpha):
    i = pl.program_id(1)

    @pl.when(i == 0)
    def _():
        prev_ref[...] = jnp.zeros_like(prev_ref)

    x1 = x1_ref[...]
    xb = x1.astype(bf16)
    row = lax.broadcasted_iota(jnp.int32, (tm, FF_CHUNK), 0)
    for c in range(D_FF // FF_CHUNK):
        hs = []
        for half in range(2):
            sl = slice(half * D_FF + c * FF_CHUNK, half * D_FF + (c + 1) * FF_CHUNK)
            up = _dot(xb, wup_ref[:, sl])
            p6 = prev_ref[6:7, sl]
            p7 = prev_ref[7:8, sl]
            s1 = jnp.where(row == 0, p7, pltpu.roll(up, 1, 0))
            s2 = jnp.where(row == 0, p6, jnp.where(row == 1, p7, pltpu.roll(up, 2, 0)))
            hs.append(cb_ref[:, sl] + up * cw_ref[2:3, sl] + s2 * cw_ref[0:1, sl] + s1 * cw_ref[1:2, sl])
            prev_ref[:, sl] = up[tm - 8:tm, :]
        act_ref[:, c * FF_CHUNK:(c + 1) * FF_CHUNK] = (_gelu_tanh(hs[0]) * hs[1]).astype(bf16)
    y = _dot(act_ref[...], wdn_ref[...])
    x2_ref[...] = _layer_norm(alpha * x1 + y, g_ref[...], b_ref[...])
    cs_ref[...] = prev_ref[...]


def _ffn_prompt(x1, wup, cw, cb, wdn, g, b, batch, seq, alpha):
    nt = seq // TM
    tok = lambda bi, i: (bi * nt + i, 0)
    consts = (wup, cw, cb, wdn, g, b)
    return pl.pallas_call(
        functools.partial(_ffn_prompt_kernel, tm=TM, alpha=alpha),
        grid=(batch, nt),
        in_specs=[pl.BlockSpec((TM, D_MODEL), tok)] + [_const_spec(a) for a in consts],
        out_specs=(pl.BlockSpec((TM, D_MODEL), tok),
                   pl.BlockSpec((None, 8, 2 * D_FF), lambda bi, i: (bi, 0, 0))),
        out_shape=(jax.ShapeDtypeStruct((batch * seq, D_MODEL), f32),
                   jax.ShapeDtypeStruct((batch, 8, 2 * D_FF), f32)),
        scratch_shapes=[pltpu.VMEM((8, 2 * D_FF), f32), pltpu.VMEM((TM, D_FF), bf16)],
        compiler_params=_params(("arbitrary", "arbitrary")),
        name="ffn_prompt",
    )(x1, *map(_operand, consts))


def _ffn_sample_kernel(x1_ref, s0_ref, s1_ref, wup_ref, cw_ref, cb_ref, wdn_ref, g_ref, b_ref, x2_ref, up_ref,
                       *, alpha):
    x1 = x1_ref[...]
    up = _dot(x1.astype(bf16), wup_ref[...])
    up_ref[...] = up
    h = cb_ref[...] + up * cw_ref[2:3, :] + s0_ref[...] * cw_ref[0:1, :] + s1_ref[...] * cw_ref[1:2, :]
    act = _gelu_tanh(h[:, 0:D_FF]) * h[:, D_FF:2 * D_FF]
    y = _dot(act.astype(bf16), wdn_ref[...])
    x2_ref[...] = _layer_norm(alpha * x1 + y, g_ref[...], b_ref[...])


def _ffn_sample(x1, s0, s1, wup, cw, cb, wdn, g, b, alpha):
    args = (x1, s0, s1, wup, cw, cb, wdn, g, b)
    n = x1.shape[0]
    return pl.pallas_call(
        functools.partial(_ffn_sample_kernel, alpha=alpha),
        grid=(1,),
        in_specs=[_const_spec(a) for a in args],
        out_specs=(_full_spec(x1.shape), _full_spec((n, 2 * D_FF))),
        out_shape=(jax.ShapeDtypeStruct(x1.shape, f32), jax.ShapeDtypeStruct((n, 2 * D_FF), f32)),
        compiler_params=_params(("arbitrary",)),
        name="ffn_sample",
    )(*map(_operand, args))


def _proj_sample_kernel(x_ref, w_ref, brow_ref, q_ref, k_ref, v_ref, logf_ref, u_ref, sg_ref):
    z = _dot(x_ref[...].astype(bf16), w_ref[...])
    q_ref[...] = z[:, C_Q:C_K]
    k_ref[...] = z[:, C_K:C_V]
    v_ref[...] = z[:, C_V:C_F]
    logf_ref[...] = _log_sigmoid(z[:, C_F:C_U] + brow_ref[...])[:, 0:N_HEADS]
    u_ref[...] = z[:, C_U:C_G]
    sg_ref[...] = _sigmoid(z[:, C_G:C_END]).astype(bf16)


def _proj_sample(x, w_all, brow):
    n = x.shape[0]
    shapes = ((n, D_ATTN), (n, D_ATTN), (n, D_ATTN), (n, N_HEADS), (n, D_POOL), (n, 2 * D_MODEL))
    dtypes = (f32, f32, f32, f32, f32, bf16)
    return pl.pallas_call(
        _proj_sample_kernel,
        grid=(1,),
        in_specs=[_const_spec(x), _const_spec(w_all), _const_spec(brow)],
        out_specs=tuple(_full_spec(s) for s in shapes),
        out_shape=tuple(jax.ShapeDtypeStruct(s, d) for s, d in zip(shapes, dtypes)),
        compiler_params=_params(("arbitrary",)),
        name="proj_sample",
    )(x, _operand(w_all), _operand(brow))


def _attn_sample_kernel(pt_ref, q_ref, kn_ref, vn_ref, fn_ref, *rest, n_pages):
    k_refs = rest[0:n_pages]
    v_refs = rest[n_pages:2 * n_pages]
    f_refs = rest[2 * n_pages:3 * n_pages]
    o_ref = rest[3 * n_pages]

    lane_head = lax.broadcasted_iota(jnp.int32, (N_HEADS, D_ATTN), 1) // HEAD_DIM
    own = lane_head == lax.broadcasted_iota(jnp.int32, (N_HEADS, D_ATTN), 0)
    q = q_ref[...].astype(bf16).astype(f32)
    qbd = jnp.where(own, q, 0.0).astype(bf16)

    f_all = jnp.concatenate([f_refs[j][...] for j in range(n_pages)], axis=0)
    a = lax.broadcasted_iota(jnp.int32, (PAGE_SIZE, PAGE_SIZE), 0)
    s = lax.broadcasted_iota(jnp.int32, (PAGE_SIZE, PAGE_SIZE), 1)
    later = jnp.where(a > s, 1.0, 0.0).astype(bf16)
    hi, mid, lo = _split3(f_all)
    within = _dot(hi, later) + _dot(mid, later) + _dot(lo, later)
    running = fn_ref[...]
    logits = [None] * n_pages
    for j in reversed(range(n_pages)):
        sc = _dot(qbd, k_refs[j][...].astype(bf16))
        logits[j] = sc + LOG2E * (within[j * N_HEADS:(j + 1) * N_HEADS, :] + running)
        running = running + jnp.sum(f_refs[j][...], axis=1, keepdims=True)

    kn = kn_ref[...].astype(bf16).astype(f32)
    s_new = jnp.sum(jnp.where(own, q * kn, 0.0), axis=1, keepdims=True)
    m = s_new
    for j in range(n_pages):
        m = jnp.maximum(m, jnp.max(logits[j], axis=1, keepdims=True))
    p_new = jnp.exp2(s_new - m)
    l = p_new
    acc = p_new * vn_ref[...].astype(bf16).astype(f32)
    for j in range(n_pages):
        p = jnp.exp2(logits[j] - m)
        l = l + jnp.sum(p, axis=1, keepdims=True)
        acc = acc + lax.dot_general(p.astype(bf16), v_refs[j][...].astype(bf16), NT_DIMS,
                                    preferred_element_type=f32)
    o_ref[...] = jnp.sum(jnp.where(own, acc / l, 0.0), axis=0, keepdims=True)


def _attn_sample(page_table, q, kn, vn, fn, cache_kt, cache_vt, cache_ft, layer):
    n, n_pages = page_table.shape
    row = pl.BlockSpec((None, 1, D_ATTN), lambda b, pt: (b, 0, 0))
    kv_specs = [pl.BlockSpec((None, None, D_ATTN, PAGE_SIZE), lambda b, pt, j=j: (layer, pt[b, j], 0, 0))
                for j in range(n_pages)]
    f_specs = [pl.BlockSpec((None, None, N_HEADS, PAGE_SIZE), lambda b, pt, j=j: (layer, pt[b, j], 0, 0))
               for j in range(n_pages)]
    grid_spec = pltpu.PrefetchScalarGridSpec(
        num_scalar_prefetch=1,
        grid=(n,),
        in_specs=[row, row, row, pl.BlockSpec((None, N_HEADS, 1), lambda b, pt: (b, 0, 0))]
        + kv_specs + kv_specs + f_specs,
        out_specs=row,
    )
    return pl.pallas_call(
        functools.partial(_attn_sample_kernel, n_pages=n_pages),
        grid_spec=grid_spec,
        out_shape=jax.ShapeDtypeStruct((n, 1, D_ATTN), f32),
        compiler_params=_params(("arbitrary",)),
        name="attn_sample",
    )(page_table, q, kn, vn, fn, *([cache_kt] * n_pages), *([cache_vt] * n_pages), *([cache_ft] * n_pages))


def kernel(x_prompt, x_sample, cache_k, cache_v, cache_logf, state_pool, state_conv, page_table, w_in, b_forget,
           w_pool, pool_scale, w_proj_a, w_proj_b, w_out, ln1_g, ln1_b, w_up, conv_w, conv_b, w_down, ln2_g, ln2_b):
    depth = w_in.shape[0]
    bp, sp, _ = x_prompt.shape
    bs, ss, _ = x_sample.shape
    assert ss == 1 and sp % TM == 0 and sp % TQ == 0 and sp >= POOL_STATE
    n_pool = cache_k.shape[1]
    alpha = (2 * depth) ** 0.25

    wq = w_in[:, :, 0:D_ATTN] * (HEAD_DIM ** -0.5 * LOG2E)
    wk = w_in[:, :, D_ATTN:2 * D_ATTN]
    wv = w_in[:, :, 2 * D_ATTN:3 * D_ATTN]
    o0 = 3 * D_ATTN
    wf = w_in[:, :, o0:o0 + N_HEADS]
    wrest = w_in[:, :, o0 + N_HEADS:]
    wf_pad = jnp.pad(wf, ((0, 0), (0, 0), (0, LANE - N_HEADS)))
    w_all = jnp.concatenate([wq, wk, wv, wf_pad, wrest], axis=2).astype(bf16)
    tr = lambda a: jnp.swapaxes(a, 1, 2)
    wf_t = jnp.pad(tr(wf), ((0, 0), (0, F_PAD - N_HEADS), (0, 0)))
    w_t = jnp.concatenate([tr(wq), tr(wk), tr(wv), wf_t], axis=1).astype(bf16)
    brow = jnp.pad(b_forget, ((0, 0), (0, LANE - N_HEADS)))[:, None, :]
    bcol = jnp.pad(b_forget, ((0, 0), (0, F_PAD - N_HEADS)))[:, :, None]
    placement = _decay_placement()
    wpool_b = w_pool.astype(bf16)
    wpa_b, wpb_b, wout_b = w_proj_a.astype(bf16), w_proj_b.astype(bf16), w_out.astype(bf16)
    wup_b, wdn_b = w_up.astype(bf16), w_down.astype(bf16)
    row = lambda a: a[:, None, :]
    scale_r, g1, b1, g2, b2, cb_r = row(pool_scale), row(ln1_g), row(ln1_b), row(ln2_g), row(ln2_b), row(conv_b)

    ckt = jnp.transpose(cache_k, (0, 1, 3, 4, 2)).reshape(depth, n_pool, D_ATTN, PAGE_SIZE)
    cvt = jnp.transpose(cache_v, (0, 1, 3, 4, 2)).reshape(depth, n_pool, D_ATTN, PAGE_SIZE)
    cft = jnp.swapaxes(cache_logf, 2, 3)
    st_pool_t = jnp.swapaxes(state_pool, 1, 2)

    xp = x_prompt.reshape(bp * sp, D_MODEL)
    xs = x_sample.reshape(bs, D_MODEL)
    kp, vp, fp, pp, cp = [], [], [], [], []
    ksm, vsm, fsm, psm, csm = [], [], [], [], []
    for l in range(depth):
        at = lambda a: _Layered(a, l)
        merge_w = (at(wpool_b), at(scale_r), at(wpa_b), at(wpb_b), at(wout_b), at(g1), at(b1))
        ffn_w = (at(wup_b), at(conv_w), at(cb_r), at(wdn_b), at(g2), at(b2))
        qT, qaug, kT, kb, kaug, vT, vTb, logfT, u, sg = _proj_prompt(xp, at(w_all), at(w_t), at(brow), at(bcol),
                                                                     placement, bp, sp)
        oT = _attn_prompt(qT, qaug, kb, kaug, vTb, bp, sp)
        x1 = _merge_prompt(xp, oT, u, sg, *merge_w, bp, sp, alpha)
        xp, cs = _ffn_prompt(x1, *ffn_w, bp, sp, alpha)
        kp.append(kT)
        vp.append(vT)
        fp.append(logfT)
        pp.append(u.reshape(bp, sp, D_POOL)[:, sp - POOL_STATE:])
        cp.append(cs[:, 8 - CONV_STATE:])

        q_s, k_s, v_s, f_s, u_s, sg_s = _proj_sample(xs, at(w_all), at(brow))
        oa = _attn_sample(page_table, q_s[:, None, :], k_s[:, None, :], v_s[:, None, :], f_s[:, :, None],
                          ckt, cvt, cft, l)
        x1s = _merge_sample(xs, oa.reshape(bs, D_ATTN), u_s, at(st_pool_t), sg_s, *merge_w, alpha)
        xs, up_s = _ffn_sample(x1s, state_conv[l, :, 0], state_conv[l, :, 1], *ffn_w, alpha)
        ksm.append(k_s.reshape(bs, 1, N_HEADS, HEAD_DIM))
        vsm.append(v_s.reshape(bs, 1, N_HEADS, HEAD_DIM))
        fsm.append(f_s.reshape(bs, 1, N_HEADS))
        psm.append(jnp.concatenate([state_pool[l][:, 1:], u_s[:, None, :]], axis=1))
        csm.append(jnp.concatenate([state_conv[l][:, 1:], up_s[:, None, :]], axis=1))

    heads_last = lambda t: jnp.transpose(jnp.stack(t).reshape(depth, bp, N_HEADS, HEAD_DIM, sp), (0, 1, 4, 2, 3))
    return (xp.reshape(bp, sp, D_MODEL), xs.reshape(bs, 1, D_MODEL),
            heads_last(kp), heads_last(vp), jnp.transpose(jnp.stack(fp), (0, 1, 3, 2)), jnp.stack(pp), jnp.stack(cp),
            jnp.stack(ksm), jnp.stack(vsm), jnp.stack(fsm), jnp.stack(psm), jnp.stack(csm))
```

```python
import functools
from typing import NamedTuple

import numpy as np
import jax
import jax.numpy as jnp
from jax import lax
from jax.experimental import pallas as pl
from jax.experimental.pallas import tpu as pltpu

D_MODEL = 1024
N_HEADS = 8
HEAD_DIM = 64
D_ATTN = N_HEADS * HEAD_DIM
D_POOL = D_MODEL // 2
POOL_WINDOWS = (2, 4, 8, 16)
POOL_GROUP = D_POOL // len(POOL_WINDOWS)
POOL_STATE = max(POOL_WINDOWS) - 1
POOL_HALO = 16
D_FF = 2816
CONV_STATE = 2
PAGE_SIZE = 128
LN_EPS = 1e-5
NEG_INF = -1e30
LANE = 128
F_PAD = 16
AUG = LANE // N_HEADS
V_PAD = 16
QK_AHEAD = 3
LOG2E = 1.4426950408889634

C_Q, C_K, C_V = 0, D_ATTN, 2 * D_ATTN
C_F = 3 * D_ATTN
C_U = C_F + LANE
C_G = C_U + D_POOL
C_END = C_G + 2 * D_MODEL
R_F = 3 * D_ATTN
R_END = R_F + F_PAD

TM = 512
TQ = 512
FF_CHUNK = 256
VMEM_LIMIT = 56 * 1024 * 1024

f32 = jnp.float32
bf16 = jnp.bfloat16
NT_DIMS = (((1,), (1,)), ((), ()))
TN_DIMS = (((0,), (0,)), ((), ()))


def _dot(a, b):
    return jnp.dot(a, b, preferred_element_type=f32)


def _log_sigmoid(x):
    return jnp.minimum(x, 0.0) - jnp.log1p(jnp.exp(-jnp.abs(x)))


def _sigmoid(x):
    return 1.0 / (1.0 + jnp.exp(-x))


def _gelu_tanh(x):
    return 0.5 * x * (1.0 + jnp.tanh(0.7978845608028654 * (x + 0.044715 * (x * x * x))))


def _layer_norm(y, g, b):
    mu = jnp.mean(y, axis=-1, keepdims=True)
    d = y - mu
    var = jnp.mean(d * d, axis=-1, keepdims=True)
    return d * lax.rsqrt(var + LN_EPS) * g + b


def _split3(x):
    hi = x.astype(bf16)
    r1 = x - hi.astype(f32)
    mid = r1.astype(bf16)
    lo = (r1 - mid.astype(f32)).astype(bf16)
    return hi, mid, lo


class _Layered(NamedTuple):
    array: jax.Array
    layer: int


def _operand(a):
    return a.array if isinstance(a, _Layered) else a


def _const_spec(a):
    if isinstance(a, _Layered):
        shape, layer = a.array.shape[1:], a.layer
        return pl.BlockSpec((None,) + shape, lambda *_: (layer,) + (0,) * len(shape), pipeline_mode=pl.Buffered(1))
    nd = len(a.shape)
    return pl.BlockSpec(a.shape, lambda *_: (0,) * nd, pipeline_mode=pl.Buffered(1))


def _full_spec(shape):
    nd = len(shape)
    return pl.BlockSpec(shape, lambda *_: (0,) * nd)


def _params(semantics):
    return pltpu.CompilerParams(dimension_semantics=semantics, vmem_limit_bytes=VMEM_LIMIT)


def _decay_placement():
    e_c = np.zeros((3 * LANE, LANE), np.float32)
    e_r = np.zeros((LANE, 3 * F_PAD), np.float32)
    ones_c = np.zeros((1, LANE), np.float32)
    ones_r = np.zeros((LANE, 1), np.float32)
    for h in range(N_HEADS):
        for p in range(3):
            e_c[p * LANE + h, h * AUG + p] = -1.0
            e_r[h * AUG + 3 + p, p * F_PAD + h] = 1.0
            ones_c[0, h * AUG + 3 + p] = 1.0
            ones_r[h * AUG + p, 0] = 1.0
    return jnp.asarray(e_c, bf16), jnp.asarray(ones_c), jnp.asarray(e_r, bf16), jnp.asarray(ones_r)


def _proj_prompt_kernel(x_ref, w_ref, wt_ref, brow_ref, bcol_ref, ec_ref, onec_ref, er_ref, oner_ref,
                        qT_ref, qaug_ref, kT_ref, kb_ref, kaug_ref, vT_ref, vTb_ref, logfT_ref, u_ref, sg_ref,
                        carry_c, carry_r, *, tm):
    i = pl.program_id(1)

    @pl.when(i == 0)
    def _():
        carry_c[...] = jnp.zeros_like(carry_c)
        carry_r[...] = jnp.zeros_like(carry_r)

    xb = x_ref[...].astype(bf16)
    zt = lax.dot_general(wt_ref[...], xb, NT_DIMS, preferred_element_type=f32)
    qT_ref[...] = zt[0:D_ATTN].astype(bf16)
    kt = zt[D_ATTN:2 * D_ATTN]
    kT_ref[...] = kt
    kb_ref[...] = kt.T.astype(bf16)
    vt = zt[2 * D_ATTN:R_F]
    vT_ref[...] = vt
    vTb_ref[...] = vt.astype(bf16)
    lf_row = _log_sigmoid(zt[R_F:R_END] + bcol_ref[...])
    logfT_ref[...] = lf_row[0:N_HEADS]
    lf_col = _log_sigmoid(_dot(xb, w_ref[:, C_F:C_U]) + brow_ref[...])

    r = lax.broadcasted_iota(jnp.int32, (tm, tm), 0)
    c = lax.broadcasted_iota(jnp.int32, (tm, tm), 1)
    lower = jnp.where(c <= r, 1.0, 0.0).astype(bf16)
    upper = jnp.where(r <= c, 1.0, 0.0).astype(bf16)
    cc = _dot(lower, jnp.concatenate(_split3(lf_col), axis=1))
    cum_c = cc[:, 0:LANE] + cc[:, LANE:2 * LANE] + cc[:, 2 * LANE:3 * LANE] + carry_c[...]
    carry_c[...] = cum_c[tm - 1:tm, :]
    kaug_ref[...] = (_dot(jnp.concatenate(_split3(cum_c * LOG2E), axis=1), ec_ref[...]) + onec_ref[...]).astype(bf16)
    cr = _dot(jnp.concatenate(_split3(lf_row), axis=0), upper)
    cum_r = cr[0:F_PAD] + cr[F_PAD:2 * F_PAD] + cr[2 * F_PAD:3 * F_PAD] + carry_r[:, 0:1]
    carry_r[...] = jnp.broadcast_to(cum_r[:, tm - 1:tm], carry_r.shape)
    qaug_ref[...] = (_dot(er_ref[...], jnp.concatenate(_split3(cum_r * LOG2E), axis=0)) + oner_ref[...]).astype(bf16)

    u_ref[...] = _dot(xb, w_ref[:, C_U:C_G])
    sg_ref[...] = _sigmoid(_dot(xb, w_ref[:, C_G:C_END])).astype(bf16)


def _proj_prompt(x, w_all, w_t, brow, bcol, placement, batch, seq):
    m = batch * seq
    nt = seq // TM
    tok = lambda b, i: (b * nt + i, 0)
    tr = lambda b, i: (b, 0, i)
    out_shape = (
        jax.ShapeDtypeStruct((batch, D_ATTN, seq), bf16),
        jax.ShapeDtypeStruct((batch, LANE, seq), bf16),
        jax.ShapeDtypeStruct((batch, D_ATTN, seq), f32),
        jax.ShapeDtypeStruct((m, D_ATTN), bf16),
        jax.ShapeDtypeStruct((m, LANE), bf16),
        jax.ShapeDtypeStruct((batch, D_ATTN, seq), f32),
        jax.ShapeDtypeStruct((batch, D_ATTN, seq), bf16),
        jax.ShapeDtypeStruct((batch, N_HEADS, seq), f32),
        jax.ShapeDtypeStruct((m, D_POOL), f32),
        jax.ShapeDtypeStruct((m, 2 * D_MODEL), bf16),
    )
    out_specs = (
        pl.BlockSpec((None, D_ATTN, TM), tr),
        pl.BlockSpec((None, LANE, TM), tr),
        pl.BlockSpec((None, D_ATTN, TM), tr),
        pl.BlockSpec((TM, D_ATTN), tok),
        pl.BlockSpec((TM, LANE), tok),
        pl.BlockSpec((None, D_ATTN, TM), tr),
        pl.BlockSpec((None, D_ATTN, TM), tr),
        pl.BlockSpec((None, N_HEADS, TM), tr),
        pl.BlockSpec((TM, D_POOL), tok),
        pl.BlockSpec((TM, 2 * D_MODEL), tok),
    )
    consts = (w_all, w_t, brow, bcol) + tuple(placement)
    return pl.pallas_call(
        functools.partial(_proj_prompt_kernel, tm=TM),
        grid=(batch, nt),
        in_specs=[pl.BlockSpec((TM, D_MODEL), tok)] + [_const_spec(a) for a in consts],
        out_specs=out_specs,
        out_shape=out_shape,
        scratch_shapes=[pltpu.VMEM((1, LANE), f32), pltpu.VMEM((F_PAD, LANE), f32)],
        compiler_params=_params(("arbitrary", "arbitrary")),
        name="proj_prompt",
    )(x, *map(_operand, consts))


def _prompt_attention_part(qT_ref, qaug_ref, kb_ref, kaug_ref, vT_ref, oT_ref, m_ref, acc_ref, i, part, *, tq, parts):
    r = lax.broadcasted_iota(jnp.int32, (tq, tq), 0)
    c = lax.broadcasted_iota(jnp.int32, (tq, tq), 1)
    causal = r <= c
    zeros = jnp.zeros((HEAD_DIM, tq), bf16)
    ones = jnp.ones((V_PAD, tq), bf16)
    slot_head = lax.broadcasted_iota(jnp.int32, (LANE, tq), 0) // AUG
    qa = qaug_ref[...].astype(f32)

    rhs = []
    for h in range(N_HEADS):
        qh = qT_ref[h * HEAD_DIM:(h + 1) * HEAD_DIM, :]
        aug = jnp.where(slot_head == h, qa, 0.0).astype(bf16)
        rhs.append(jnp.concatenate(([qh, zeros] if h % 2 == 0 else [zeros, qh]) + [aug], axis=0))

    @pl.when(part == 0)
    def _():
        m_ref[...] = jnp.full(m_ref.shape, NEG_INF, f32)
        acc_ref[...] = jnp.zeros(acc_ref.shape, f32)

    def step(j, masked):
        off = pl.multiple_of(j * tq, tq)
        ka = kaug_ref[pl.ds(off, tq), :]

        def scores(h):
            pair = h // 2
            lhs = jnp.concatenate([kb_ref[pl.ds(off, tq), pair * LANE:(pair + 1) * LANE], ka], axis=1)
            return _dot(lhs, rhs[h])

        ss = {h: scores(h) for h in range(QK_AHEAD)}
        for h in range(N_HEADS):
            if h + QK_AHEAD < N_HEADS:
                ss[h + QK_AHEAD] = scores(h + QK_AHEAD)
            s = ss.pop(h)
            if masked:
                s = jnp.where(causal, s, NEG_INF)
            m = m_ref[h]
            m_new = jnp.maximum(m, jnp.max(s, axis=0, keepdims=True))
            alpha = jnp.exp2(m - m_new)
            p = jnp.exp2(s - m_new).astype(bf16)
            va = jnp.concatenate([vT_ref[h * HEAD_DIM:(h + 1) * HEAD_DIM, pl.ds(off, tq)], ones], axis=0)
            acc_ref[h] = alpha * acc_ref[h] + _dot(va, p)
            m_ref[h] = m_new

    def body(j, carry):
        step(j, False)
        return carry

    n_tiles = i + 1
    lo = (n_tiles * part) // parts
    hi = (n_tiles * (part + 1)) // parts
    lax.fori_loop(lo, jnp.minimum(hi, i), body, 0)

    @pl.when(part == parts - 1)
    def _():
        step(i, True)
        for h in range(N_HEADS):
            acc = acc_ref[h]
            oT_ref[h * HEAD_DIM:(h + 1) * HEAD_DIM, :] = (acc[0:HEAD_DIM] / acc[HEAD_DIM:HEAD_DIM + 1]).astype(bf16)


def _pool_project(pooled, wpool_ref, scale_ref):
    outs = [_dot(pooled[g].astype(bf16), wpool_ref[g]) for g in range(len(POOL_WINDOWS))]
    return jnp.concatenate(outs, axis=-1) * scale_ref[...]


def _merge_tail(x, pa, ob, sg_ref, wpb_ref, wout_ref, g_ref, b_ref, alpha):
    pb = _dot(ob.astype(bf16), wpb_ref[...])
    merged = sg_ref[:, 0:D_MODEL] * pa + sg_ref[:, D_MODEL:2 * D_MODEL] * pb
    mix = _dot(merged.astype(bf16), wout_ref[...])
    return _layer_norm(alpha * x + mix, g_ref[...], b_ref[...])


def _merge_prompt_kernel(x_ref, oT_ref, u_ref, uh_ref, sg_ref, wpool_ref, scale_ref, wpa_ref, wpb_ref, wout_ref,
                         g_ref, b_ref, x1_ref, *, tm, alpha):
    i = pl.program_id(1)
    u = u_ref[...]
    halo = jnp.where(i > 0, uh_ref[...], 0.0)
    uext = jnp.concatenate([halo, u], axis=0)
    pos = i * tm + lax.broadcasted_iota(jnp.int32, (tm, 1), 0)
    pooled = []
    for g, w in enumerate(POOL_WINDOWS):
        sl = slice(g * POOL_GROUP, (g + 1) * POOL_GROUP)
        a = uext[:, sl]
        for st in range(g + 1):
            a = a + pltpu.roll(a, 2 ** st, 0)
        cnt = jnp.minimum(pos + 1, w).astype(f32)
        pooled.append(a[POOL_HALO:, :] / cnt - u[:, sl])
    ob = _pool_project(pooled, wpool_ref, scale_ref)
    pa = lax.dot_general(oT_ref[...], wpa_ref[...], TN_DIMS, preferred_element_type=f32)
    x1_ref[...] = _merge_tail(x_ref[...], pa, ob, sg_ref, wpb_ref, wout_ref, g_ref, b_ref, alpha)


def _merge_prompt(x, oT, u, sg, wpool, scale, wpa, wpb, wout, g, b, batch, seq, alpha):
    nt = seq // TM
    hb = TM // POOL_HALO
    tok = lambda bi, i: (bi * nt + i, 0)
    halo = lambda bi, i: (bi * (seq // POOL_HALO) + jnp.maximum(i * hb - 1, 0), 0)
    consts = (wpool, scale, wpa, wpb, wout, g, b)
    return pl.pallas_call(
        functools.partial(_merge_prompt_kernel, tm=TM, alpha=alpha),
        grid=(batch, nt),
        in_specs=[
            pl.BlockSpec((TM, D_MODEL), tok),
            pl.BlockSpec((None, D_ATTN, TM), lambda bi, i: (bi, 0, i)),
            pl.BlockSpec((TM, D_POOL), tok),
            pl.BlockSpec((POOL_HALO, D_POOL), halo),
            pl.BlockSpec((TM, 2 * D_MODEL), tok),
        ] + [_const_spec(a) for a in consts],
        out_specs=pl.BlockSpec((TM, D_MODEL), tok),
        out_shape=jax.ShapeDtypeStruct((batch * seq, D_MODEL), f32),
        compiler_params=_params(("arbitrary", "arbitrary")),
        name="merge_prompt",
    )(x, oT, u, u, sg, *map(_operand, consts))


def _merge_sample_kernel(x_ref, oa_ref, u_ref, st_ref, sg_ref, wpool_ref, scale_ref, wpa_ref, wpb_ref, wout_ref,
                         g_ref, b_ref, x1_ref, *, alpha):
    u = u_ref[...]
    pooled = []
    for g, w in enumerate(POOL_WINDOWS):
        sl = slice(g * POOL_GROUP, (g + 1) * POOL_GROUP)
        a = u[:, sl]
        for t in range(POOL_STATE - (w - 1), POOL_STATE):
            a = a + st_ref[t, :, sl]
        pooled.append(a / float(w) - u[:, sl])
    ob = _pool_project(pooled, wpool_ref, scale_ref)
    pa = _dot(oa_ref[...].astype(bf16), wpa_ref[...])
    x1_ref[...] = _merge_tail(x_ref[...], pa, ob, sg_ref, wpb_ref, wout_ref, g_ref, b_ref, alpha)


def _merge_sample(x, oa, u, st, sg, wpool, scale, wpa, wpb, wout, g, b, alpha):
    args = (x, oa, u, st, sg, wpool, scale, wpa, wpb, wout, g, b)
    return pl.pallas_call(
        functools.partial(_merge_sample_kernel, alpha=alpha),
        grid=(1,),
        in_specs=[_const_spec(a) for a in args],
        out_specs=_full_spec(x.shape),
        out_shape=jax.ShapeDtypeStruct(x.shape, f32),
        compiler_params=_params(("arbitrary",)),
        name="merge_sample",
    )(*map(_operand, args))


def _ffn_prompt_kernel(x1_ref, wup_ref, cw_ref, cb_ref, wdn_ref, g_ref, b_ref, x2_ref, cs_ref, prev_ref, act_ref,
                       *, tm, alpha):
    i = pl.program_id(1)

    @pl.when(i == 0)
    def _():
        prev_ref[...] = jnp.zeros_like(prev_ref)

    x1 = x1_ref[...]
    xb = x1.astype(bf16)
    row = lax.broadcasted_iota(jnp.int32, (tm, FF_CHUNK), 0)
    for c in range(D_FF // FF_CHUNK):
        hs = []
        for half in range(2):
            sl = slice(half * D_FF + c * FF_CHUNK, half * D_FF + (c + 1) * FF_CHUNK)
            up = _dot(xb, wup_ref[:, sl])
            p6 = prev_ref[6:7, sl]
            p7 = prev_ref[7:8, sl]
            s1 = jnp.where(row == 0, p7, pltpu.roll(up, 1, 0))
            s2 = jnp.where(row == 0, p6, jnp.where(row == 1, p7, pltpu.roll(up, 2, 0)))
            hs.append(cb_ref[:, sl] + up * cw_ref[2:3, sl] + s2 * cw_ref[0:1, sl] + s1 * cw_ref[1:2, sl])
            prev_ref[:, sl] = up[tm - 8:tm, :]
        act_ref[:, c * FF_CHUNK:(c + 1) * FF_CHUNK] = (_gelu_tanh(hs[0]) * hs[1]).astype(bf16)
    y = _dot(act_ref[...], wdn_ref[...])
    x2_ref[...] = _layer_norm(alpha * x1 + y, g_ref[...], b_ref[...])
    cs_ref[...] = prev_ref[...]


def _ffn_prompt(x1, wup, cw, cb, wdn, g, b, batch, seq, alpha):
    nt = seq // TM
    tok = lambda bi, i: (bi * nt + i, 0)
    consts = (wup, cw, cb, wdn, g, b)
    return pl.pallas_call(
        functools.partial(_ffn_prompt_kernel, tm=TM, alpha=alpha),
        grid=(batch, nt),
        in_specs=[pl.BlockSpec((TM, D_MODEL), tok)] + [_const_spec(a) for a in consts],
        out_specs=(pl.BlockSpec((TM, D_MODEL), tok),
                   pl.BlockSpec((None, 8, 2 * D_FF), lambda bi, i: (bi, 0, 0))),
        out_shape=(jax.ShapeDtypeStruct((batch * seq, D_MODEL), f32),
                   jax.ShapeDtypeStruct((batch, 8, 2 * D_FF), f32)),
        scratch_shapes=[pltpu.VMEM((8, 2 * D_FF), f32), pltpu.VMEM((TM, D_FF), bf16)],
        compiler_params=_params(("arbitrary", "arbitrary")),
        name="ffn_prompt",
    )(x1, *map(_operand, consts))


def _ffn_sample_kernel(x1_ref, s0_ref, s1_ref, wup_ref, cw_ref, cb_ref, wdn_ref, g_ref, b_ref, x2_ref, up_ref,
                       *, alpha):
    x1 = x1_ref[...]
    up = _dot(x1.astype(bf16), wup_ref[...])
    up_ref[...] = up
    h = cb_ref[...] + up * cw_ref[2:3, :] + s0_ref[...] * cw_ref[0:1, :] + s1_ref[...] * cw_ref[1:2, :]
    act = _gelu_tanh(h[:, 0:D_FF]) * h[:, D_FF:2 * D_FF]
    y = _dot(act.astype(bf16), wdn_ref[...])
    x2_ref[...] = _layer_norm(alpha * x1 + y, g_ref[...], b_ref[...])


def _ffn_sample(x1, s0, s1, wup, cw, cb, wdn, g, b, alpha):
    args = (x1, s0, s1, wup, cw, cb, wdn, g, b)
    n = x1.shape[0]
    return pl.pallas_call(
        functools.partial(_ffn_sample_kernel, alpha=alpha),
        grid=(1,),
        in_specs=[_const_spec(a) for a in args],
        out_specs=(_full_spec(x1.shape), _full_spec((n, 2 * D_FF))),
        out_shape=(jax.ShapeDtypeStruct(x1.shape, f32), jax.ShapeDtypeStruct((n, 2 * D_FF), f32)),
        compiler_params=_params(("arbitrary",)),
        name="ffn_sample",
    )(*map(_operand, args))


def _proj_sample_kernel(x_ref, w_ref, brow_ref, q_ref, k_ref, v_ref, logf_ref, u_ref, sg_ref):
    z = _dot(x_ref[...].astype(bf16), w_ref[...])
    q_ref[...] = z[:, C_Q:C_K]
    k_ref[...] = z[:, C_K:C_V]
    v_ref[...] = z[:, C_V:C_F]
    logf_ref[...] = _log_sigmoid(z[:, C_F:C_U] + brow_ref[...])[:, 0:N_HEADS]
    u_ref[...] = z[:, C_U:C_G]
    sg_ref[...] = _sigmoid(z[:, C_G:C_END]).astype(bf16)


def _proj_sample(x, w_all, brow):
    n = x.shape[0]
    shapes = ((n, D_ATTN), (n, D_ATTN), (n, D_ATTN), (n, N_HEADS), (n, D_POOL), (n, 2 * D_MODEL))
    dtypes = (f32, f32, f32, f32, f32, bf16)
    return pl.pallas_call(
        _proj_sample_kernel,
        grid=(1,),
        in_specs=[_const_spec(x), _const_spec(w_all), _const_spec(brow)],
        out_specs=tuple(_full_spec(s) for s in shapes),
        out_shape=tuple(jax.ShapeDtypeStruct(s, d) for s, d in zip(shapes, dtypes)),
        compiler_params=_params(("arbitrary",)),
        name="proj_sample",
    )(x, _operand(w_all), _operand(brow))


def _sample_attention(q_ref, kn_ref, vn_ref, fn_ref, k_refs, v_refs, f_refs, o_ref):
    n_pages = len(k_refs)
    lane_head = lax.broadcasted_iota(jnp.int32, (N_HEADS, D_ATTN), 1) // HEAD_DIM
    own = lane_head == lax.broadcasted_iota(jnp.int32, (N_HEADS, D_ATTN), 0)
    q = q_ref[...].astype(bf16).astype(f32)
    qbd = jnp.where(own, q, 0.0).astype(bf16)

    f_all = jnp.concatenate([f_refs[j][...] for j in range(n_pages)], axis=0)
    a = lax.broadcasted_iota(jnp.int32, (PAGE_SIZE, PAGE_SIZE), 0)
    s = lax.broadcasted_iota(jnp.int32, (PAGE_SIZE, PAGE_SIZE), 1)
    later = jnp.where(a > s, 1.0, 0.0).astype(bf16)
    hi, mid, lo = _split3(f_all)
    within = _dot(hi, later) + _dot(mid, later) + _dot(lo, later)
    running = fn_ref[...]
    decay = [None] * n_pages
    for j in reversed(range(n_pages)):
        decay[j] = within[j * N_HEADS:(j + 1) * N_HEADS, :] + running
        running = running + jnp.sum(f_refs[j][...], axis=1, keepdims=True)
    k_all = jnp.concatenate([k_refs[j][...].astype(bf16) for j in range(n_pages)], axis=1)
    logits = _dot(qbd, k_all) + LOG2E * jnp.concatenate(decay, axis=1)

    kn = kn_ref[...].astype(bf16).astype(f32)
    s_new = jnp.sum(jnp.where(own, q * kn, 0.0), axis=1, keepdims=True)
    m = jnp.maximum(s_new, jnp.max(logits, axis=1, keepdims=True))
    p_new = jnp.exp2(s_new - m)
    p = jnp.exp2(logits - m)
    l = p_new + jnp.sum(p, axis=1, keepdims=True)
    v_all = jnp.concatenate([v_refs[j][...].astype(bf16) for j in range(n_pages)], axis=1)
    acc = p_new * vn_ref[...].astype(bf16).astype(f32) + lax.dot_general(
        p.astype(bf16), v_all, NT_DIMS, preferred_element_type=f32)
    o_ref[...] = jnp.sum(jnp.where(own, acc / l, 0.0), axis=0, keepdims=True)


def _attn_kernel(pt_ref, qT_ref, qaug_ref, kb_ref, kaug_ref, vT_ref, q_ref, kn_ref, vn_ref, fn_ref, *rest,
                 tq, n_pages, parts):
    k_refs = rest[0:n_pages]
    v_refs = rest[n_pages:2 * n_pages]
    f_refs = rest[2 * n_pages:3 * n_pages]
    oT_ref, os_ref, m_ref, acc_ref = rest[3 * n_pages:]
    _sample_attention(q_ref, kn_ref, vn_ref, fn_ref, k_refs, v_refs, f_refs, os_ref)
    _prompt_attention_part(qT_ref, qaug_ref, kb_ref, kaug_ref, vT_ref, oT_ref, m_ref, acc_ref,
                           pl.program_id(1), pl.program_id(2), tq=tq, parts=parts)


def _attention(qT, qaug, kb, kaug, vTb, batch, seq, page_table, q, kn, vn, fn, cache_kt, cache_vt, cache_ft, layer):
    nq = seq // TQ
    n, n_pages = page_table.shape
    parts = n // (batch * nq)
    assert parts * batch * nq == n and parts >= 1
    sample = lambda b, i, p: (b * nq + i) * parts + p
    row = pl.BlockSpec((None, 1, D_ATTN), lambda b, i, p, pt: (sample(b, i, p), 0, 0))
    kv_specs = [pl.BlockSpec((None, None, D_ATTN, PAGE_SIZE),
                             lambda b, i, p, pt, j=j: (layer, pt[sample(b, i, p), j], 0, 0)) for j in range(n_pages)]
    f_specs = [pl.BlockSpec((None, None, N_HEADS, PAGE_SIZE),
                            lambda b, i, p, pt, j=j: (layer, pt[sample(b, i, p), j], 0, 0)) for j in range(n_pages)]
    q_tile = lambda b, i, p, pt: (b, 0, i)
    grid_spec = pltpu.PrefetchScalarGridSpec(
        num_scalar_prefetch=1,
        grid=(batch, nq, parts),
        in_specs=[
            pl.BlockSpec((None, D_ATTN, TQ), q_tile),
            pl.BlockSpec((None, LANE, TQ), q_tile),
            pl.BlockSpec((seq, D_ATTN), lambda b, i, p, pt: (b, 0)),
            pl.BlockSpec((seq, LANE), lambda b, i, p, pt: (b, 0)),
            pl.BlockSpec((None, D_ATTN, seq), lambda b, i, p, pt: (b, 0, 0)),
            row, row, row, pl.BlockSpec((None, N_HEADS, 1), lambda b, i, p, pt: (sample(b, i, p), 0, 0)),
        ] + kv_specs + kv_specs + f_specs,
        out_specs=(pl.BlockSpec((None, D_ATTN, TQ), q_tile), row),
        scratch_shapes=[pltpu.VMEM((N_HEADS, 1, TQ), f32),
                        pltpu.VMEM((N_HEADS, HEAD_DIM + V_PAD, TQ), f32)],
    )
    return pl.pallas_call(
        functools.partial(_attn_kernel, tq=TQ, n_pages=n_pages, parts=parts),
        grid_spec=grid_spec,
        out_shape=(jax.ShapeDtypeStruct((batch, D_ATTN, seq), bf16), jax.ShapeDtypeStruct((n, 1, D_ATTN), f32)),
        compiler_params=_params(("arbitrary", "arbitrary", "arbitrary")),
        name="attention",
    )(page_table, qT, qaug, kb, kaug, vTb, q, kn, vn, fn,
      *([cache_kt] * n_pages), *([cache_vt] * n_pages), *([cache_ft] * n_pages))


def kernel(x_prompt, x_sample, cache_k, cache_v, cache_logf, state_pool, state_conv, page_table, w_in, b_forget,
           w_pool, pool_scale, w_proj_a, w_proj_b, w_out, ln1_g, ln1_b, w_up, conv_w, conv_b, w_down, ln2_g, ln2_b):
    depth = w_in.shape[0]
    bp, sp, _ = x_prompt.shape
    bs, ss, _ = x_sample.shape
    assert ss == 1 and sp % TM == 0 and sp % TQ == 0 and sp >= POOL_STATE
    n_pool = cache_k.shape[1]
    alpha = (2 * depth) ** 0.25

    wq = w_in[:, :, 0:D_ATTN] * (HEAD_DIM ** -0.5 * LOG2E)
    wk = w_in[:, :, D_ATTN:2 * D_ATTN]
    wv = w_in[:, :, 2 * D_ATTN:3 * D_ATTN]
    o0 = 3 * D_ATTN
    wf = w_in[:, :, o0:o0 + N_HEADS]
    wrest = w_in[:, :, o0 + N_HEADS:]
    wf_pad = jnp.pad(wf, ((0, 0), (0, 0), (0, LANE - N_HEADS)))
    w_all = jnp.concatenate([wq, wk, wv, wf_pad, wrest], axis=2).astype(bf16)
    tr = lambda a: jnp.swapaxes(a, 1, 2)
    wf_t = jnp.pad(tr(wf), ((0, 0), (0, F_PAD - N_HEADS), (0, 0)))
    w_t = jnp.concatenate([tr(wq), tr(wk), tr(wv), wf_t], axis=1).astype(bf16)
    brow = jnp.pad(b_forget, ((0, 0), (0, LANE - N_HEADS)))[:, None, :]
    bcol = jnp.pad(b_forget, ((0, 0), (0, F_PAD - N_HEADS)))[:, :, None]
    placement = _decay_placement()
    wpool_b = w_pool.astype(bf16)
    wpa_b, wpb_b, wout_b = w_proj_a.astype(bf16), w_proj_b.astype(bf16), w_out.astype(bf16)
    wup_b, wdn_b = w_up.astype(bf16), w_down.astype(bf16)
    row = lambda a: a[:, None, :]
    scale_r, g1, b1, g2, b2, cb_r = row(pool_scale), row(ln1_g), row(ln1_b), row(ln2_g), row(ln2_b), row(conv_b)

    ckt = jnp.transpose(cache_k, (0, 1, 3, 4, 2)).reshape(depth, n_pool, D_ATTN, PAGE_SIZE)
    cvt = jnp.transpose(cache_v, (0, 1, 3, 4, 2)).reshape(depth, n_pool, D_ATTN, PAGE_SIZE)
    cft = jnp.swapaxes(cache_logf, 2, 3)
    st_pool_t = jnp.swapaxes(state_pool, 1, 2)

    xp = x_prompt.reshape(bp * sp, D_MODEL)
    xs = x_sample.reshape(bs, D_MODEL)
    kp, vp, fp, pp, cp = [], [], [], [], []
    ksm, vsm, fsm, psm, csm = [], [], [], [], []
    for l in range(depth):
        at = lambda a: _Layered(a, l)
        merge_w = (at(wpool_b), at(scale_r), at(wpa_b), at(wpb_b), at(wout_b), at(g1), at(b1))
        ffn_w = (at(wup_b), at(conv_w), at(cb_r), at(wdn_b), at(g2), at(b2))
        qT, qaug, kT, kb, kaug, vT, vTb, logfT, u, sg = _proj_prompt(xp, at(w_all), at(w_t), at(brow), at(bcol),
                                                                     placement, bp, sp)
        q_s, k_s, v_s, f_s, u_s, sg_s = _proj_sample(xs, at(w_all), at(brow))
        oT, oa = _attention(qT, qaug, kb, kaug, vTb, bp, sp, page_table, q_s[:, None, :], k_s[:, None, :],
                            v_s[:, None, :], f_s[:, :, None], ckt, cvt, cft, l)

        x1 = _merge_prompt(xp, oT, u, sg, *merge_w, bp, sp, alpha)
        xp, cs = _ffn_prompt(x1, *ffn_w, bp, sp, alpha)
        kp.append(kT)
        vp.append(vT)
        fp.append(logfT)
        pp.append(u.reshape(bp, sp, D_POOL)[:, sp - POOL_STATE:])
        cp.append(cs[:, 8 - CONV_STATE:])

        x1s = _merge_sample(xs, oa.reshape(bs, D_ATTN), u_s, at(st_pool_t), sg_s, *merge_w, alpha)
        xs, up_s = _ffn_sample(x1s, state_conv[l, :, 0], state_conv[l, :, 1], *ffn_w, alpha)
        ksm.append(k_s.reshape(bs, 1, N_HEADS, HEAD_DIM))
        vsm.append(v_s.reshape(bs, 1, N_HEADS, HEAD_DIM))
        fsm.append(f_s.reshape(bs, 1, N_HEADS))
        psm.append(jnp.concatenate([state_pool[l][:, 1:], u_s[:, None, :]], axis=1))
        csm.append(jnp.concatenate([state_conv[l][:, 1:], up_s[:, None, :]], axis=1))

    heads_last = lambda t: jnp.transpose(jnp.stack(t).reshape(depth, bp, N_HEADS, HEAD_DIM, sp), (0, 1, 4, 2, 3))
    return (xp.reshape(bp, sp, D_MODEL), xs.reshape(bs, 1, D_MODEL),
            heads_last(kp), heads_last(vp), jnp.transpose(jnp.stack(fp), (0, 1, 3, 2)), jnp.stack(pp), jnp.stack(cp),
            jnp.stack(ksm), jnp.stack(vsm), jnp.stack(fsm), jnp.stack(psm), jnp.stack(csm))
```

```python
import functools
from typing import NamedTuple

import numpy as np
import jax
import jax.numpy as jnp
from jax import lax
from jax.experimental import pallas as pl
from jax.experimental.pallas import tpu as pltpu

D_MODEL = 1024
N_HEADS = 8
HEAD_DIM = 64
D_ATTN = N_HEADS * HEAD_DIM
D_POOL = D_MODEL // 2
POOL_WINDOWS = (2, 4, 8, 16)
POOL_GROUP = D_POOL // len(POOL_WINDOWS)
POOL_STATE = max(POOL_WINDOWS) - 1
POOL_HALO = 16
D_FF = 2816
CONV_STATE = 2
PAGE_SIZE = 128
LN_EPS = 1e-5
NEG_INF = -1e30
LANE = 128
F_PAD = 16
AUG = LANE // N_HEADS
V_PAD = 16
QK_AHEAD = 3
LOG2E = 1.4426950408889634

C_Q, C_K, C_V = 0, D_ATTN, 2 * D_ATTN
C_F = 3 * D_ATTN
C_U = C_F + LANE
C_G = C_U + D_POOL
C_END = C_G + 2 * D_MODEL
R_F = 3 * D_ATTN
R_END = R_F + F_PAD

TM = 512
TQ = 512
FF_CHUNK = 256
VMEM_LIMIT = 56 * 1024 * 1024

f32 = jnp.float32
bf16 = jnp.bfloat16
NT_DIMS = (((1,), (1,)), ((), ()))
TN_DIMS = (((0,), (0,)), ((), ()))


def _dot(a, b):
    return jnp.dot(a, b, preferred_element_type=f32)


def _log_sigmoid(x):
    return jnp.minimum(x, 0.0) - jnp.log1p(jnp.exp(-jnp.abs(x)))


def _sigmoid(x):
    return 1.0 / (1.0 + jnp.exp(-x))


def _gelu_tanh(x):
    return 0.5 * x * (1.0 + jnp.tanh(0.7978845608028654 * (x + 0.044715 * (x * x * x))))


def _layer_norm(y, g, b):
    mu = jnp.mean(y, axis=-1, keepdims=True)
    d = y - mu
    var = jnp.mean(d * d, axis=-1, keepdims=True)
    return d * lax.rsqrt(var + LN_EPS) * g + b


def _split3(x):
    hi = x.astype(bf16)
    r1 = x - hi.astype(f32)
    mid = r1.astype(bf16)
    lo = (r1 - mid.astype(f32)).astype(bf16)
    return hi, mid, lo


class _Layered(NamedTuple):
    array: jax.Array
    layer: int


def _operand(a):
    return a.array if isinstance(a, _Layered) else a


def _const_spec(a):
    if isinstance(a, _Layered):
        shape, layer = a.array.shape[1:], a.layer
        return pl.BlockSpec((None,) + shape, lambda *_: (layer,) + (0,) * len(shape), pipeline_mode=pl.Buffered(1))
    nd = len(a.shape)
    return pl.BlockSpec(a.shape, lambda *_: (0,) * nd, pipeline_mode=pl.Buffered(1))


def _full_spec(shape):
    nd = len(shape)
    return pl.BlockSpec(shape, lambda *_: (0,) * nd)


def _params(semantics):
    return pltpu.CompilerParams(dimension_semantics=semantics, vmem_limit_bytes=VMEM_LIMIT)


def _decay_placement():
    e_c = np.zeros((3 * LANE, LANE), np.float32)
    e_r = np.zeros((LANE, 3 * F_PAD), np.float32)
    ones_c = np.zeros((1, LANE), np.float32)
    ones_r = np.zeros((LANE, 1), np.float32)
    for h in range(N_HEADS):
        for p in range(3):
            e_c[p * LANE + h, h * AUG + p] = -1.0
            e_r[h * AUG + 3 + p, p * F_PAD + h] = 1.0
            ones_c[0, h * AUG + 3 + p] = 1.0
            ones_r[h * AUG + p, 0] = 1.0
    return jnp.asarray(e_c, bf16), jnp.asarray(ones_c), jnp.asarray(e_r, bf16), jnp.asarray(ones_r)


def _proj_prompt_kernel(x_ref, w_ref, wt_ref, brow_ref, bcol_ref, ec_ref, onec_ref, er_ref, oner_ref,
                        qT_ref, qaug_ref, kT_ref, kb_ref, kaug_ref, vT_ref, vTb_ref, logfT_ref, u_ref, sg_ref,
                        carry_c, carry_r, *, tm):
    i = pl.program_id(1)

    @pl.when(i == 0)
    def _():
        carry_c[...] = jnp.zeros_like(carry_c)
        carry_r[...] = jnp.zeros_like(carry_r)

    xb = x_ref[...].astype(bf16)
    zt = lax.dot_general(wt_ref[...], xb, NT_DIMS, preferred_element_type=f32)
    qT_ref[...] = zt[0:D_ATTN].astype(bf16)
    kt = zt[D_ATTN:2 * D_ATTN]
    kT_ref[...] = kt
    kb_ref[...] = kt.T.astype(bf16)
    vt = zt[2 * D_ATTN:R_F]
    vT_ref[...] = vt
    vTb_ref[...] = vt.astype(bf16)
    lf_row = _log_sigmoid(zt[R_F:R_END] + bcol_ref[...])
    logfT_ref[...] = lf_row[0:N_HEADS]
    lf_col = _log_sigmoid(_dot(xb, w_ref[:, C_F:C_U]) + brow_ref[...])

    r = lax.broadcasted_iota(jnp.int32, (tm, tm), 0)
    c = lax.broadcasted_iota(jnp.int32, (tm, tm), 1)
    lower = jnp.where(c <= r, 1.0, 0.0).astype(bf16)
    upper = jnp.where(r <= c, 1.0, 0.0).astype(bf16)
    cc = _dot(lower, jnp.concatenate(_split3(lf_col), axis=1))
    cum_c = cc[:, 0:LANE] + cc[:, LANE:2 * LANE] + cc[:, 2 * LANE:3 * LANE] + carry_c[...]
    carry_c[...] = cum_c[tm - 1:tm, :]
    kaug_ref[...] = (_dot(jnp.concatenate(_split3(cum_c * LOG2E), axis=1), ec_ref[...]) + onec_ref[...]).astype(bf16)
    cr = _dot(jnp.concatenate(_split3(lf_row), axis=0), upper)
    cum_r = cr[0:F_PAD] + cr[F_PAD:2 * F_PAD] + cr[2 * F_PAD:3 * F_PAD] + carry_r[:, 0:1]
    carry_r[...] = jnp.broadcast_to(cum_r[:, tm - 1:tm], carry_r.shape)
    qaug_ref[...] = (_dot(er_ref[...], jnp.concatenate(_split3(cum_r * LOG2E), axis=0)) + oner_ref[...]).astype(bf16)

    u_ref[...] = _dot(xb, w_ref[:, C_U:C_G])
    sg_ref[...] = _sigmoid(_dot(xb, w_ref[:, C_G:C_END])).astype(bf16)


def _proj_prompt(x, w_all, w_t, brow, bcol, placement, batch, seq):
    m = batch * seq
    nt = seq // TM
    tok = lambda b, i: (b * nt + i, 0)
    tr = lambda b, i: (b, 0, i)
    out_shape = (
        jax.ShapeDtypeStruct((batch, D_ATTN, seq), bf16),
        jax.ShapeDtypeStruct((batch, LANE, seq), bf16),
        jax.ShapeDtypeStruct((batch, D_ATTN, seq), f32),
        jax.ShapeDtypeStruct((m, D_ATTN), bf16),
        jax.ShapeDtypeStruct((m, LANE), bf16),
        jax.ShapeDtypeStruct((batch, D_ATTN, seq), f32),
        jax.ShapeDtypeStruct((batch, D_ATTN, seq), bf16),
        jax.ShapeDtypeStruct((batch, N_HEADS, seq), f32),
        jax.ShapeDtypeStruct((m, D_POOL), f32),
        jax.ShapeDtypeStruct((m, 2 * D_MODEL), bf16),
    )
    out_specs = (
        pl.BlockSpec((None, D_ATTN, TM), tr),
        pl.BlockSpec((None, LANE, TM), tr),
        pl.BlockSpec((None, D_ATTN, TM), tr),
        pl.BlockSpec((TM, D_ATTN), tok),
        pl.BlockSpec((TM, LANE), tok),
        pl.BlockSpec((None, D_ATTN, TM), tr),
        pl.BlockSpec((None, D_ATTN, TM), tr),
        pl.BlockSpec((None, N_HEADS, TM), tr),
        pl.BlockSpec((TM, D_POOL), tok),
        pl.BlockSpec((TM, 2 * D_MODEL), tok),
    )
    consts = (w_all, w_t, brow, bcol) + tuple(placement)
    return pl.pallas_call(
        functools.partial(_proj_prompt_kernel, tm=TM),
        grid=(batch, nt),
        in_specs=[pl.BlockSpec((TM, D_MODEL), tok)] + [_const_spec(a) for a in consts],
        out_specs=out_specs,
        out_shape=out_shape,
        scratch_shapes=[pltpu.VMEM((1, LANE), f32), pltpu.VMEM((F_PAD, LANE), f32)],
        compiler_params=_params(("arbitrary", "arbitrary")),
        name="proj_prompt",
    )(x, *map(_operand, consts))


def _prompt_attention_part(qT_ref, qaug_ref, kb_ref, kaug_ref, vT_ref, oT_ref, m_ref, acc_ref, i, part, *, tq, parts):
    r = lax.broadcasted_iota(jnp.int32, (tq, tq), 0)
    c = lax.broadcasted_iota(jnp.int32, (tq, tq), 1)
    causal = r <= c
    zeros = jnp.zeros((HEAD_DIM, tq), bf16)
    ones = jnp.ones((V_PAD, tq), bf16)
    slot_head = lax.broadcasted_iota(jnp.int32, (LANE, tq), 0) // AUG
    qa = qaug_ref[...].astype(f32)

    rhs = []
    for h in range(N_HEADS):
        qh = qT_ref[h * HEAD_DIM:(h + 1) * HEAD_DIM, :]
        aug = jnp.where(slot_head == h, qa, 0.0).astype(bf16)
        rhs.append(jnp.concatenate(([qh, zeros] if h % 2 == 0 else [zeros, qh]) + [aug], axis=0))

    @pl.when(part == 0)
    def _():
        m_ref[...] = jnp.full(m_ref.shape, NEG_INF, f32)
        acc_ref[...] = jnp.zeros(acc_ref.shape, f32)

    def step(j, masked):
        off = pl.multiple_of(j * tq, tq)
        ka = kaug_ref[pl.ds(off, tq), :]

        def scores(h):
            pair = h // 2
            lhs = jnp.concatenate([kb_ref[pl.ds(off, tq), pair * LANE:(pair + 1) * LANE], ka], axis=1)
            return _dot(lhs, rhs[h])

        ss = {h: scores(h) for h in range(QK_AHEAD)}
        for h in range(N_HEADS):
            if h + QK_AHEAD < N_HEADS:
                ss[h + QK_AHEAD] = scores(h + QK_AHEAD)
            s = ss.pop(h)
            if masked:
                s = jnp.where(causal, s, NEG_INF)
            m = m_ref[h]
            m_new = jnp.maximum(m, jnp.max(s, axis=0, keepdims=True))
            alpha = jnp.exp2(m - m_new)
            p = jnp.exp2(s - m_new).astype(bf16)
            va = jnp.concatenate([vT_ref[h * HEAD_DIM:(h + 1) * HEAD_DIM, pl.ds(off, tq)], ones], axis=0)
            acc_ref[h] = alpha * acc_ref[h] + _dot(va, p)
            m_ref[h] = m_new

    def body(j, carry):
        step(j, False)
        return carry

    n_tiles = i + 1
    lo = (n_tiles * part) // parts
    hi = (n_tiles * (part + 1)) // parts
    lax.fori_loop(lo, jnp.minimum(hi, i), body, 0)

    @pl.when(part == parts - 1)
    def _():
        step(i, True)
        for h in range(N_HEADS):
            acc = acc_ref[h]
            oT_ref[h * HEAD_DIM:(h + 1) * HEAD_DIM, :] = (acc[0:HEAD_DIM] / acc[HEAD_DIM:HEAD_DIM + 1]).astype(bf16)


def _pool_project(pooled, wpool_ref, scale_ref):
    outs = [_dot(pooled[g].astype(bf16), wpool_ref[g]) for g in range(len(POOL_WINDOWS))]
    return jnp.concatenate(outs, axis=-1) * scale_ref[...]


def _merge_tail(x, pa, ob, sg_ref, wpb_ref, wout_ref, g_ref, b_ref, alpha):
    pb = _dot(ob.astype(bf16), wpb_ref[...])
    merged = sg_ref[:, 0:D_MODEL] * pa + sg_ref[:, D_MODEL:2 * D_MODEL] * pb
    mix = _dot(merged.astype(bf16), wout_ref[...])
    return _layer_norm(alpha * x + mix, g_ref[...], b_ref[...])


def _merge_prompt_kernel(x_ref, oT_ref, u_ref, uh_ref, sg_ref, wpool_ref, scale_ref, wpa_ref, wpb_ref, wout_ref,
                         g_ref, b_ref, x1_ref, *, tm, alpha):
    i = pl.program_id(1)
    u = u_ref[...]
    halo = jnp.where(i > 0, uh_ref[...], 0.0)
    uext = jnp.concatenate([halo, u], axis=0)
    pos = i * tm + lax.broadcasted_iota(jnp.int32, (tm, 1), 0)
    pooled = []
    for g, w in enumerate(POOL_WINDOWS):
        sl = slice(g * POOL_GROUP, (g + 1) * POOL_GROUP)
        a = uext[:, sl]
        for st in range(g + 1):
            a = a + pltpu.roll(a, 2 ** st, 0)
        cnt = jnp.minimum(pos + 1, w).astype(f32)
        pooled.append(a[POOL_HALO:, :] / cnt - u[:, sl])
    ob = _pool_project(pooled, wpool_ref, scale_ref)
    pa = lax.dot_general(oT_ref[...], wpa_ref[...], TN_DIMS, preferred_element_type=f32)
    x1_ref[...] = _merge_tail(x_ref[...], pa, ob, sg_ref, wpb_ref, wout_ref, g_ref, b_ref, alpha)


def _merge_prompt(x, oT, u, sg, wpool, scale, wpa, wpb, wout, g, b, batch, seq, alpha):
    nt = seq // TM
    hb = TM // POOL_HALO
    tok = lambda bi, i: (bi * nt + i, 0)
    halo = lambda bi, i: (bi * (seq // POOL_HALO) + jnp.maximum(i * hb - 1, 0), 0)
    consts = (wpool, scale, wpa, wpb, wout, g, b)
    return pl.pallas_call(
        functools.partial(_merge_prompt_kernel, tm=TM, alpha=alpha),
        grid=(batch, nt),
        in_specs=[
            pl.BlockSpec((TM, D_MODEL), tok),
            pl.BlockSpec((None, D_ATTN, TM), lambda bi, i: (bi, 0, i)),
            pl.BlockSpec((TM, D_POOL), tok),
            pl.BlockSpec((POOL_HALO, D_POOL), halo),
            pl.BlockSpec((TM, 2 * D_MODEL), tok),
        ] + [_const_spec(a) for a in consts],
        out_specs=pl.BlockSpec((TM, D_MODEL), tok),
        out_shape=jax.ShapeDtypeStruct((batch * seq, D_MODEL), f32),
        compiler_params=_params(("arbitrary", "arbitrary")),
        name="merge_prompt",
    )(x, oT, u, u, sg, *map(_operand, consts))


def _merge_sample_kernel(x_ref, oa_ref, u_ref, st_ref, sg_ref, wpool_ref, scale_ref, wpa_ref, wpb_ref, wout_ref,
                         g_ref, b_ref, x1_ref, *, alpha):
    u = u_ref[...]
    pooled = []
    for g, w in enumerate(POOL_WINDOWS):
        sl = slice(g * POOL_GROUP, (g + 1) * POOL_GROUP)
        a = u[:, sl]
        for t in range(POOL_STATE - (w - 1), POOL_STATE):
            a = a + st_ref[t, :, sl]
        pooled.append(a / float(w) - u[:, sl])
    ob = _pool_project(pooled, wpool_ref, scale_ref)
    pa = _dot(oa_ref[...].astype(bf16), wpa_ref[...])
    x1_ref[...] = _merge_tail(x_ref[...], pa, ob, sg_ref, wpb_ref, wout_ref, g_ref, b_ref, alpha)


def _merge_sample(x, oa, u, st, sg, wpool, scale, wpa, wpb, wout, g, b, alpha):
    args = (x, oa, u, st, sg, wpool, scale, wpa, wpb, wout, g, b)
    return pl.pallas_call(
        functools.partial(_merge_sample_kernel, alpha=alpha),
        grid=(1,),
        in_specs=[_const_spec(a) for a in args],
        out_specs=_full_spec(x.shape),
        out_shape=jax.ShapeDtypeStruct(x.shape, f32),
        compiler_params=_params(("arbitrary",)),
        name="merge_sample",
    )(*map(_operand, args))


def _ffn_prompt_kernel(x1_ref, wup_ref, cw_ref, cb_ref, wdn_ref, g_ref, b_ref, x2_ref, cs_ref, prev_ref, act_ref,
                       *, tm, alpha):
    i = pl.program_id(1)

    @pl.when(i == 0)
    def _():
        prev_ref[...] = jnp.zeros_like(prev_ref)

    x1 = x1_ref[...]
    xb = x1.astype(bf16)
    row = lax.broadcasted_iota(jnp.int32, (tm, FF_CHUNK), 0)
    for c in range(D_FF // FF_CHUNK):
        hs = []
        for half in range(2):
            sl = slice(half * D_FF + c * FF_CHUNK, half * D_FF + (c + 1) * FF_CHUNK)
            up = _dot(xb, wup_ref[:, sl])
            p6 = prev_ref[6:7, sl]
            p7 = prev_ref[7:8, sl]
            s1 = jnp.where(row == 0, p7, pltpu.roll(up, 1, 0))
            s2 = jnp.where(row == 0, p6, jnp.where(row == 1, p7, pltpu.roll(up, 2, 0)))
            hs.append(cb_ref[:, sl] + up * cw_ref[2:3, sl] + s2 * cw_ref[0:1, sl] + s1 * cw_ref[1:2, sl])
            prev_ref[:, sl] = up[tm - 8:tm, :]
        act_ref[:, c * FF_CHUNK:(c + 1) * FF_CHUNK] = (_gelu_tanh(hs[0]) * hs[1]).astype(bf16)
    y = _dot(act_ref[...], wdn_ref[...])
    x2_ref[...] = _layer_norm(alpha * x1 + y, g_ref[...], b_ref[...])
    cs_ref[...] = prev_ref[...]


def _ffn_prompt(x1, wup, cw, cb, wdn, g, b, batch, seq, alpha):
    nt = seq // TM
    tok = lambda bi, i: (bi * nt + i, 0)
    consts = (wup, cw, cb, wdn, g, b)
    return pl.pallas_call(
        functools.partial(_ffn_prompt_kernel, tm=TM, alpha=alpha),
        grid=(batch, nt),
        in_specs=[pl.BlockSpec((TM, D_MODEL), tok)] + [_const_spec(a) for a in consts],
        out_specs=(pl.BlockSpec((TM, D_MODEL), tok),
                   pl.BlockSpec((None, 8, 2 * D_FF), lambda bi, i: (bi, 0, 0))),
        out_shape=(jax.ShapeDtypeStruct((batch * seq, D_MODEL), f32),
                   jax.ShapeDtypeStruct((batch, 8, 2 * D_FF), f32)),
        scratch_shapes=[pltpu.VMEM((8, 2 * D_FF), f32), pltpu.VMEM((TM, D_FF), bf16)],
        compiler_params=_params(("arbitrary", "arbitrary")),
        name="ffn_prompt",
    )(x1, *map(_operand, consts))


def _ffn_sample_kernel(x1_ref, s0_ref, s1_ref, wup_ref, cw_ref, cb_ref, wdn_ref, g_ref, b_ref, x2_ref, up_ref,
                       *, alpha):
    x1 = x1_ref[...]
    up = _dot(x1.astype(bf16), wup_ref[...])
    up_ref[...] = up
    h = cb_ref[...] + up * cw_ref[2:3, :] + s0_ref[...] * cw_ref[0:1, :] + s1_ref[...] * cw_ref[1:2, :]
    act = _gelu_tanh(h[:, 0:D_FF]) * h[:, D_FF:2 * D_FF]
    y = _dot(act.astype(bf16), wdn_ref[...])
    x2_ref[...] = _layer_norm(alpha * x1 + y, g_ref[...], b_ref[...])


def _ffn_sample(x1, s0, s1, wup, cw, cb, wdn, g, b, alpha):
    args = (x1, s0, s1, wup, cw, cb, wdn, g, b)
    n = x1.shape[0]
    return pl.pallas_call(
        functools.partial(_ffn_sample_kernel, alpha=alpha),
        grid=(1,),
        in_specs=[_const_spec(a) for a in args],
        out_specs=(_full_spec(x1.shape), _full_spec((n, 2 * D_FF))),
        out_shape=(jax.ShapeDtypeStruct(x1.shape, f32), jax.ShapeDtypeStruct((n, 2 * D_FF), f32)),
        compiler_params=_params(("arbitrary",)),
        name="ffn_sample",
    )(*map(_operand, args))


def _proj_sample_kernel(x_ref, w_ref, brow_ref, q_ref, k_ref, v_ref, logf_ref, u_ref, sg_ref):
    z = _dot(x_ref[...].astype(bf16), w_ref[...])
    q_ref[...] = z[:, C_Q:C_K]
    k_ref[...] = z[:, C_K:C_V]
    v_ref[...] = z[:, C_V:C_F]
    logf_ref[...] = _log_sigmoid(z[:, C_F:C_U] + brow_ref[...])[:, 0:N_HEADS]
    u_ref[...] = z[:, C_U:C_G]
    sg_ref[...] = _sigmoid(z[:, C_G:C_END]).astype(bf16)


def _proj_sample(x, w_all, brow):
    n = x.shape[0]
    shapes = ((n, D_ATTN), (n, D_ATTN), (n, D_ATTN), (n, N_HEADS), (n, D_POOL), (n, 2 * D_MODEL))
    dtypes = (f32, f32, f32, f32, f32, bf16)
    return pl.pallas_call(
        _proj_sample_kernel,
        grid=(1,),
        in_specs=[_const_spec(x), _const_spec(w_all), _const_spec(brow)],
        out_specs=tuple(_full_spec(s) for s in shapes),
        out_shape=tuple(jax.ShapeDtypeStruct(s, d) for s, d in zip(shapes, dtypes)),
        compiler_params=_params(("arbitrary",)),
        name="proj_sample",
    )(x, _operand(w_all), _operand(brow))


def _sample_attention(q_ref, kn_ref, vn_ref, fn_ref, k_refs, v_refs, f_refs, o_ref):
    n_pages = len(k_refs)
    lane_head = lax.broadcasted_iota(jnp.int32, (N_HEADS, D_ATTN), 1) // HEAD_DIM
    own = lane_head == lax.broadcasted_iota(jnp.int32, (N_HEADS, D_ATTN), 0)
    q = q_ref[...].astype(bf16).astype(f32)
    qbd = jnp.where(own, q, 0.0).astype(bf16)

    f_all = jnp.concatenate([f_refs[j][...] for j in range(n_pages)], axis=0)
    a = lax.broadcasted_iota(jnp.int32, (PAGE_SIZE, PAGE_SIZE), 0)
    s = lax.broadcasted_iota(jnp.int32, (PAGE_SIZE, PAGE_SIZE), 1)
    later = jnp.where(a > s, 1.0, 0.0).astype(bf16)
    hi, mid, lo = _split3(f_all)
    within = _dot(hi, later) + _dot(mid, later) + _dot(lo, later)
    running = fn_ref[...]
    decay = [None] * n_pages
    for j in reversed(range(n_pages)):
        decay[j] = within[j * N_HEADS:(j + 1) * N_HEADS, :] + running
        running = running + jnp.sum(f_refs[j][...], axis=1, keepdims=True)
    k_all = jnp.concatenate([k_refs[j][...].astype(bf16) for j in range(n_pages)], axis=1)
    logits = _dot(qbd, k_all) + LOG2E * jnp.concatenate(decay, axis=1)

    kn = kn_ref[...].astype(bf16).astype(f32)
    s_new = jnp.sum(jnp.where(own, q * kn, 0.0), axis=1, keepdims=True)
    m = jnp.maximum(s_new, jnp.max(logits, axis=1, keepdims=True))
    p_new = jnp.exp2(s_new - m)
    p = jnp.exp2(logits - m)
    l = p_new + jnp.sum(p, axis=1, keepdims=True)
    v_all = jnp.concatenate([v_refs[j][...].astype(bf16) for j in range(n_pages)], axis=1)
    acc = p_new * vn_ref[...].astype(bf16).astype(f32) + lax.dot_general(
        p.astype(bf16), v_all, NT_DIMS, preferred_element_type=f32)
    o_ref[...] = jnp.sum(jnp.where(own, acc / l, 0.0), axis=0, keepdims=True)


def _attn_kernel(pt_ref, qT_ref, qaug_ref, kb_ref, kaug_ref, vT_ref, q_ref, kn_ref, vn_ref, fn_ref,
                 ck_hbm, cv_hbm, cf_hbm, oT_ref, os_ref, m_ref, acc_ref, kbuf, vbuf, fbuf, sem,
                 *, tq, n_pages, parts, layer, n_steps):
    nq = pl.num_programs(1)
    step = (pl.program_id(0) * nq + pl.program_id(1)) * parts + pl.program_id(2)
    slot = step % 2

    def page_copies(sample, slot):
        copies = []
        for j in range(n_pages):
            page = pt_ref[sample, j]
            copies.append(pltpu.make_async_copy(ck_hbm.at[layer, page], kbuf.at[slot, j], sem.at[slot, 0]))
            copies.append(pltpu.make_async_copy(cv_hbm.at[layer, page], vbuf.at[slot, j], sem.at[slot, 1]))
            copies.append(pltpu.make_async_copy(cf_hbm.at[layer, page], fbuf.at[slot, j], sem.at[slot, 2]))
        return copies

    @pl.when(step == 0)
    def _():
        for cp in page_copies(0, 0):
            cp.start()

    @pl.when(step + 1 < n_steps)
    def _():
        for cp in page_copies(step + 1, 1 - slot):
            cp.start()

    for cp in page_copies(step, slot):
        cp.wait()
    _sample_attention(q_ref, kn_ref, vn_ref, fn_ref, [kbuf.at[slot, j] for j in range(n_pages)],
                      [vbuf.at[slot, j] for j in range(n_pages)], [fbuf.at[slot, j] for j in range(n_pages)], os_ref)
    _prompt_attention_part(qT_ref, qaug_ref, kb_ref, kaug_ref, vT_ref, oT_ref, m_ref, acc_ref,
                           pl.program_id(1), pl.program_id(2), tq=tq, parts=parts)


def _attention(qT, qaug, kb, kaug, vTb, batch, seq, page_table, q, kn, vn, fn, cache_kt, cache_vt, cache_ft, layer):
    nq = seq // TQ
    n, n_pages = page_table.shape
    parts = n // (batch * nq)
    assert parts * batch * nq == n and parts >= 1
    sample = lambda b, i, p: (b * nq + i) * parts + p
    row = pl.BlockSpec((None, 1, D_ATTN), lambda b, i, p, pt: (sample(b, i, p), 0, 0))
    hbm = pl.BlockSpec(memory_space=pl.ANY)
    q_tile = lambda b, i, p, pt: (b, 0, i)
    grid_spec = pltpu.PrefetchScalarGridSpec(
        num_scalar_prefetch=1,
        grid=(batch, nq, parts),
        in_specs=[
            pl.BlockSpec((None, D_ATTN, TQ), q_tile),
            pl.BlockSpec((None, LANE, TQ), q_tile),
            pl.BlockSpec((seq, D_ATTN), lambda b, i, p, pt: (b, 0)),
            pl.BlockSpec((seq, LANE), lambda b, i, p, pt: (b, 0)),
            pl.BlockSpec((None, D_ATTN, seq), lambda b, i, p, pt: (b, 0, 0)),
            row, row, row, pl.BlockSpec((None, N_HEADS, 1), lambda b, i, p, pt: (sample(b, i, p), 0, 0)),
            hbm, hbm, hbm,
        ],
        out_specs=(pl.BlockSpec((None, D_ATTN, TQ), q_tile), row),
        scratch_shapes=[pltpu.VMEM((N_HEADS, 1, TQ), f32),
                        pltpu.VMEM((N_HEADS, HEAD_DIM + V_PAD, TQ), f32),
                        pltpu.VMEM((2, n_pages, D_ATTN, PAGE_SIZE), f32),
                        pltpu.VMEM((2, n_pages, D_ATTN, PAGE_SIZE), f32),
                        pltpu.VMEM((2, n_pages, N_HEADS, PAGE_SIZE), f32),
                        pltpu.SemaphoreType.DMA((2, 3))],
    )
    return pl.pallas_call(
        functools.partial(_attn_kernel, tq=TQ, n_pages=n_pages, parts=parts, layer=layer, n_steps=n),
        grid_spec=grid_spec,
        out_shape=(jax.ShapeDtypeStruct((batch, D_ATTN, seq), bf16), jax.ShapeDtypeStruct((n, 1, D_ATTN), f32)),
        compiler_params=_params(("arbitrary", "arbitrary", "arbitrary")),
        name="attention",
    )(page_table, qT, qaug, kb, kaug, vTb, q, kn, vn, fn, cache_kt, cache_vt, cache_ft)


def kernel(x_prompt, x_sample, cache_k, cache_v, cache_logf, state_pool, state_conv, page_table, w_in, b_forget,
           w_pool, pool_scale, w_proj_a, w_proj_b, w_out, ln1_g, ln1_b, w_up, conv_w, conv_b, w_down, ln2_g, ln2_b):
    depth = w_in.shape[0]
    bp, sp, _ = x_prompt.shape
    bs, ss, _ = x_sample.shape
    assert ss == 1 and sp % TM == 0 and sp % TQ == 0 and sp >= POOL_STATE
    n_pool = cache_k.shape[1]
    alpha = (2 * depth) ** 0.25

    wq = w_in[:, :, 0:D_ATTN] * (HEAD_DIM ** -0.5 * LOG2E)
    wk = w_in[:, :, D_ATTN:2 * D_ATTN]
    wv = w_in[:, :, 2 * D_ATTN:3 * D_ATTN]
    o0 = 3 * D_ATTN
    wf = w_in[:, :, o0:o0 + N_HEADS]
    wrest = w_in[:, :, o0 + N_HEADS:]
    wf_pad = jnp.pad(wf, ((0, 0), (0, 0), (0, LANE - N_HEADS)))
    w_all = jnp.concatenate([wq, wk, wv, wf_pad, wrest], axis=2).astype(bf16)
    tr = lambda a: jnp.swapaxes(a, 1, 2)
    wf_t = jnp.pad(tr(wf), ((0, 0), (0, F_PAD - N_HEADS), (0, 0)))
    w_t = jnp.concatenate([tr(wq), tr(wk), tr(wv), wf_t], axis=1).astype(bf16)
    brow = jnp.pad(b_forget, ((0, 0), (0, LANE - N_HEADS)))[:, None, :]
    bcol = jnp.pad(b_forget, ((0, 0), (0, F_PAD - N_HEADS)))[:, :, None]
    placement = _decay_placement()
    wpool_b = w_pool.astype(bf16)
    wpa_b, wpb_b, wout_b = w_proj_a.astype(bf16), w_proj_b.astype(bf16), w_out.astype(bf16)
    wup_b, wdn_b = w_up.astype(bf16), w_down.astype(bf16)
    row = lambda a: a[:, None, :]
    scale_r, g1, b1, g2, b2, cb_r = row(pool_scale), row(ln1_g), row(ln1_b), row(ln2_g), row(ln2_b), row(conv_b)

    ckt = jnp.transpose(cache_k, (0, 1, 3, 4, 2)).reshape(depth, n_pool, D_ATTN, PAGE_SIZE)
    cvt = jnp.transpose(cache_v, (0, 1, 3, 4, 2)).reshape(depth, n_pool, D_ATTN, PAGE_SIZE)
    cft = jnp.swapaxes(cache_logf, 2, 3)
    st_pool_t = jnp.swapaxes(state_pool, 1, 2)

    xp = x_prompt.reshape(bp * sp, D_MODEL)
    xs = x_sample.reshape(bs, D_MODEL)
    kp, vp, fp, pp, cp = [], [], [], [], []
    ksm, vsm, fsm, psm, csm = [], [], [], [], []
    for l in range(depth):
        at = lambda a: _Layered(a, l)
        merge_w = (at(wpool_b), at(scale_r), at(wpa_b), at(wpb_b), at(wout_b), at(g1), at(b1))
        ffn_w = (at(wup_b), at(conv_w), at(cb_r), at(wdn_b), at(g2), at(b2))
        qT, qaug, kT, kb, kaug, vT, vTb, logfT, u, sg = _proj_prompt(xp, at(w_all), at(w_t), at(brow), at(bcol),
                                                                     placement, bp, sp)
        q_s, k_s, v_s, f_s, u_s, sg_s = _proj_sample(xs, at(w_all), at(brow))
        oT, oa = _attention(qT, qaug, kb, kaug, vTb, bp, sp, page_table, q_s[:, None, :], k_s[:, None, :],
                            v_s[:, None, :], f_s[:, :, None], ckt, cvt, cft, l)

        x1 = _merge_prompt(xp, oT, u, sg, *merge_w, bp, sp, alpha)
        xp, cs = _ffn_prompt(x1, *ffn_w, bp, sp, alpha)
        kp.append(kT)
        vp.append(vT)
        fp.append(logfT)
        pp.append(u.reshape(bp, sp, D_POOL)[:, sp - POOL_STATE:])
        cp.append(cs[:, 8 - CONV_STATE:])

        x1s = _merge_sample(xs, oa.reshape(bs, D_ATTN), u_s, at(st_pool_t), sg_s, *merge_w, alpha)
        xs, up_s = _ffn_sample(x1s, state_conv[l, :, 0], state_conv[l, :, 1], *ffn_w, alpha)
        ksm.append(k_s.reshape(bs, 1, N_HEADS, HEAD_DIM))
        vsm.append(v_s.reshape(bs, 1, N_HEADS, HEAD_DIM))
        fsm.append(f_s.reshape(bs, 1, N_HEADS))
        psm.append(jnp.concatenate([state_pool[l][:, 1:], u_s[:, None, :]], axis=1))
        csm.append(jnp.concatenate([state_conv[l][:, 1:], up_s[:, None, :]], axis=1))

    heads_last = lambda t: jnp.transpose(jnp.stack(t).reshape(depth, bp, N_HEADS, HEAD_DIM, sp), (0, 1, 4, 2, 3))
    return (xp.reshape(bp, sp, D_MODEL), xs.reshape(bs, 1, D_MODEL),
            heads_last(kp), heads_last(vp), jnp.transpose(jnp.stack(fp), (0, 1, 3, 2)), jnp.stack(pp), jnp.stack(cp),
            jnp.stack(ksm), jnp.stack(vsm), jnp.stack(fsm), jnp.stack(psm), jnp.stack(csm))
```
